```python
import math
import jax, jax.numpy as jnp
from jax import lax
import numpy as np

D_MODEL = 4096
BATCH = 4
SEQ = 4096
DEPTH = 1

CHUNK = 64
EPS = 1e-6
GDN_HEADS = 16
GDN_DK = 128
GDN_DV = 128
CONV_K = 4
GDN_QK_W = GDN_HEADS * GDN_DK
GDN_V_W = GDN_HEADS * GDN_DV
GDN_CONV_W = 2 * GDN_QK_W + GDN_V_W
FOX_HEADS = 16
FOX_DH = 128
FOX_W = FOX_HEADS * FOX_DH
Q_BLOCK = 128
D_FF = -(-8 * D_MODEL // (3 * 256)) * 256
IN_SPLITS = (GDN_QK_W, GDN_QK_W, GDN_V_W, GDN_V_W, GDN_HEADS, GDN_HEADS,
             FOX_W, FOX_W, FOX_W, FOX_HEADS, D_MODEL, D_MODEL)
N_IN = sum(IN_SPLITS)

kernel_name = "hybrid_gdn_fox_gated_merge_swiglu"


def rmsnorm(x, w):
    xf = x.astype(jnp.float32)
    y = xf * lax.rsqrt(jnp.mean(xf * xf, axis=-1, keepdims=True) + EPS) * w.astype(jnp.float32)
    return y.astype(x.dtype)


def l2norm(x):
    return x * lax.rsqrt(jnp.sum(x * x, axis=-1, keepdims=True) + EPS)


def split_cols(p):
    idx, acc = [], 0
    for s in IN_SPLITS[:-1]:
        acc += s
        idx.append(acc)
    return jnp.split(p, idx, axis=-1)


def causal_conv(x, w):
    T = x.shape[1]
    xp = jnp.pad(x, ((0, 0), (CONV_K - 1, 0), (0, 0)))
    y = xp[:, 0:T] * w[0]
    for i in range(1, CONV_K):
        y = y + xp[:, i:i + T] * w[i]
    return y


def gated_delta_rule(q, k, v, g, beta):
    f32 = jnp.float32
    q, k, v, g, beta = (a.astype(f32) for a in (q, k, v, g, beta))
    Bn, T, H, DK = q.shape
    DV = v.shape[-1]
    N = T // CHUNK

    def to_chunks(a):
        a = jnp.moveaxis(a, 2, 1)
        return a.reshape(a.shape[:2] + (N, CHUNK) + a.shape[3:])

    qc = to_chunks(l2norm(q) * DK ** -0.5)
    kc = to_chunks(l2norm(k))
    vc = to_chunks(v)
    bc = to_chunks(beta)
    gc = jnp.cumsum(to_chunks(g), axis=-1)

    tril_incl = jnp.tril(jnp.ones((CHUNK, CHUNK), bool))
    tril_strict = jnp.tril(jnp.ones((CHUNK, CHUNK), bool), k=-1)
    decay = jnp.exp(jnp.where(tril_incl, gc[..., :, None] - gc[..., None, :], -jnp.inf))

    a_strict = jnp.where(tril_strict,
                         jnp.einsum('bhncd,bhnmd->bhncm', kc, kc) * decay * bc[..., :, None], 0.0)
    eye = jnp.eye(CHUNK, dtype=f32)
    rhs = jnp.concatenate([vc * bc[..., None], kc * (bc * jnp.exp(gc))[..., None]], axis=-1)
    uw = lax.linalg.triangular_solve(a_strict + eye, rhs, left_side=True, lower=True)
    u, w = uw[..., :DV], uw[..., DV:]
    qk = jnp.einsum('bhncd,bhnmd->bhncm', qc, kc) * decay

    def step(S, inp):
        q_i, k_i, u_i, w_i, g_i, qk_i = inp
        v_new = u_i - jnp.einsum('bhcd,bhdv->bhcv', w_i, S)
        o = (jnp.einsum('bhcd,bhdv->bhcv', q_i * jnp.exp(g_i)[..., None], S)
             + jnp.einsum('bhcm,bhmv->bhcv', qk_i, v_new))
        g_last = g_i[..., -1]
        S = (S * jnp.exp(g_last)[..., None, None]
             + jnp.einsum('bhcd,bhcv->bhdv', k_i * jnp.exp(g_last[..., None] - g_i)[..., None], v_new))
        return S, o

    xs = tuple(jnp.moveaxis(a, 2, 0) for a in (qc, kc, u, w, gc, qk))
    S0 = jnp.zeros((Bn, H, DK, DV), f32)
    _, o = lax.scan(step, S0, xs)
    o = jnp.transpose(o, (1, 0, 3, 2, 4))
    return o.reshape(Bn, T, H, DV)


def forgetting_attention(q, k, v, f_logit):
    T = q.shape[1]
    scale = FOX_DH ** -0.5
    c = jnp.cumsum(jax.nn.log_sigmoid(f_logit.astype(jnp.float32)), axis=1)
    q, k, v = (jnp.swapaxes(a, 1, 2) for a in (q, k, v))
    c = jnp.swapaxes(c, 1, 2)
    outs = []
    for i in range(T // Q_BLOCK):
        lo, hi = i * Q_BLOCK, (i + 1) * Q_BLOCK
        s = jnp.einsum('bhqd,bhkd->bhqk', q[:, :, lo:hi], k[:, :, :hi]).astype(jnp.float32) * scale
        s = s + c[:, :, lo:hi, None] - c[:, :, None, :hi]
        causal = jnp.arange(lo, hi)[:, None] >= jnp.arange(hi)[None, :]
        p = jax.nn.softmax(jnp.where(causal, s, -jnp.inf), axis=-1)
        outs.append(jnp.einsum('bhqk,bhkd->bhqd', p.astype(v.dtype), v[:, :, :hi]))
    o = jnp.concatenate(outs, axis=2)
    return jnp.swapaxes(o, 1, 2)


def hybrid_layer(x, norm_mix_w, w_in, conv_w, a_log, dt_bias, gdn_norm_w, fox_b_f,
                 fox_q_norm_w, fox_k_norm_w, w_branch_a, w_branch_b, w_out,
                 norm_ffn_w, w_ffn_gate, w_ffn_up, w_ffn_down):
    Bn, T, _ = x.shape
    xn = rmsnorm(x, norm_mix_w)
    (qa, ka, va, za, beta_logit, alpha_logit, qb, kb, vb, f_logit,
     gate_a_logit, gate_b_logit) = split_cols(xn @ w_in)

    qkv_a = jax.nn.silu(causal_conv(jnp.concatenate([qa, ka, va], axis=-1), conv_w))
    qa, ka, va = jnp.split(qkv_a, [GDN_QK_W, 2 * GDN_QK_W], axis=-1)
    beta = jax.nn.sigmoid(beta_logit.astype(jnp.float32))
    g = -jnp.exp(a_log.astype(jnp.float32)) * jax.nn.softplus(
        alpha_logit.astype(jnp.float32) + dt_bias.astype(jnp.float32))
    o_a = gated_delta_rule(qa.reshape(Bn, T, GDN_HEADS, GDN_DK), ka.reshape(Bn, T, GDN_HEADS, GDN_DK),
                           va.reshape(Bn, T, GDN_HEADS, GDN_DV), g, beta)
    o_a = rmsnorm(o_a, gdn_norm_w) * jax.nn.silu(za.reshape(Bn, T, GDN_HEADS, GDN_DV).astype(jnp.float32))
    y_a = o_a.astype(x.dtype).reshape(Bn, T, GDN_V_W) @ w_branch_a

    qb = rmsnorm(qb.reshape(Bn, T, FOX_HEADS, FOX_DH), fox_q_norm_w)
    kb = rmsnorm(kb.reshape(Bn, T, FOX_HEADS, FOX_DH), fox_k_norm_w)
    o_b = forgetting_attention(qb, kb, vb.reshape(Bn, T, FOX_HEADS, FOX_DH), f_logit + fox_b_f)
    y_b = o_b.reshape(Bn, T, FOX_W) @ w_branch_b

    merged = jax.nn.sigmoid(gate_a_logit) * y_a + jax.nn.sigmoid(gate_b_logit) * y_b
    h = x + merged @ w_out

    hn = rmsnorm(h, norm_ffn_w)
    return h + (jax.nn.silu(hn @ w_ffn_gate) * (hn @ w_ffn_up)) @ w_ffn_down


def setup_inputs(seed: int = 0) -> dict:
    key = jax.random.key(seed)
    ks = jax.random.split(key, 18)
    L = DEPTH
    f32 = jnp.float32

    def nrm(k, shape, fan_in):
        return jax.random.normal(k, shape, f32) * fan_in ** -0.5

    def gain(k, n):
        return 1.0 + 0.02 * jax.random.normal(k, (L, n), f32)

    dt = jnp.exp(jax.random.uniform(ks[6], (L, GDN_HEADS), f32, math.log(1e-3), math.log(1e-1)))
    return {
        "x": jax.random.normal(ks[0], (BATCH, SEQ, D_MODEL), f32),
        "norm_mix_w": gain(ks[1], D_MODEL),
        "w_in": nrm(ks[2], (L, D_MODEL, N_IN), D_MODEL),
        "conv_w": nrm(ks[3], (L, CONV_K, GDN_CONV_W), CONV_K),
        "a_log": jnp.log(jax.random.uniform(ks[4], (L, GDN_HEADS), f32, 1.0, 16.0)),
        "dt_bias": dt + jnp.log(-jnp.expm1(-dt)),
        "gdn_norm_w": gain(ks[5], GDN_DV),
        "fox_b_f": jax.random.uniform(ks[7], (L, FOX_HEADS), f32, 1.0, 4.0),
        "fox_q_norm_w": gain(ks[8], FOX_DH),
        "fox_k_norm_w": gain(ks[9], FOX_DH),
        "w_branch_a": nrm(ks[10], (L, GDN_V_W, D_MODEL), GDN_V_W),
        "w_branch_b": nrm(ks[11], (L, FOX_W, D_MODEL), FOX_W),
        "w_out": nrm(ks[12], (L, D_MODEL, D_MODEL), D_MODEL),
        "norm_ffn_w": gain(ks[13], D_MODEL),
        "w_ffn_gate": nrm(ks[14], (L, D_MODEL, D_FF), D_MODEL),
        "w_ffn_up": nrm(ks[15], (L, D_MODEL, D_FF), D_MODEL),
        "w_ffn_down": nrm(ks[16], (L, D_FF, D_MODEL), D_FF),
    }


def reference(x, norm_mix_w, w_in, conv_w, a_log, dt_bias, gdn_norm_w, fox_b_f,
              fox_q_norm_w, fox_k_norm_w, w_branch_a, w_branch_b, w_out,
              norm_ffn_w, w_ffn_gate, w_ffn_up, w_ffn_down):
    h = x
    for l in range(DEPTH):
        h = hybrid_layer(h, norm_mix_w[l], w_in[l], conv_w[l], a_log[l], dt_bias[l], gdn_norm_w[l],
                         fox_b_f[l], fox_q_norm_w[l], fox_k_norm_w[l], w_branch_a[l], w_branch_b[l],
                         w_out[l], norm_ffn_w[l], w_ffn_gate[l], w_ffn_up[l], w_ffn_down[l])
    return h
```

```python
import functools

import jax
import jax.numpy as jnp
from jax import lax
from jax.experimental import pallas as pl
from jax.experimental.pallas import tpu as pltpu

F32 = jnp.float32
BF16 = jnp.bfloat16

EPS = 1e-6
HEADS = 16
DH = 128
HW = HEADS * DH
CONV_K = 4
CHUNK = 64

V7X_VMEM_BYTES = 64 * 1024 * 1024
LANES = 128
SUBLANES = 8


def _vmem_limit(nbytes):
    return int(min(nbytes + (16 << 20), V7X_VMEM_BYTES - (4 << 20)))


def _pick(n, prefs):
    for p in prefs:
        if n % p == 0:
            return p
    return n


def _sigmoid(x):
    return 1.0 / (1.0 + jnp.exp(-x))


def _silu(x):
    return x * _sigmoid(x)


def _softplus(x):
    return jnp.maximum(x, 0.0) + jnp.log(1.0 + jnp.exp(-jnp.abs(x)))


def _dot(a, b):
    return jnp.dot(a, b, preferred_element_type=F32)


def _dot_nt(a, b):
    return lax.dot_general(a, b, (((1,), (1,)), ((), ())), preferred_element_type=F32)


def _dot_tn(a, b):
    return lax.dot_general(a, b, (((0,), (0,)), ((), ())), preferred_element_type=F32)


def _split2(a):
    hi = a.astype(BF16)
    lo = (a - hi.astype(F32)).astype(BF16)
    return hi, lo


def _dot_x3(a, b):
    ah, al = _split2(a)
    bh, bl = _split2(b)
    return _dot(ah, bh) + (_dot(ah, bl) + _dot(al, bh))


def _rmsnorm_kernel(x_ref, w_ref, o_ref):
    x = x_ref[...]
    ms = jnp.mean(x * x, axis=-1, keepdims=True)
    o_ref[...] = (x * lax.rsqrt(ms + EPS) * w_ref[...]).astype(o_ref.dtype)


def _rmsnorm(x, w, name):
    m, d = x.shape
    bm = _pick(m, (256, 128, 64, 8))
    return pl.pallas_call(
        _rmsnorm_kernel,
        grid=(m // bm,),
        in_specs=[pl.BlockSpec((bm, d), lambda i: (i, 0)),
                  pl.BlockSpec((1, d), lambda i: (0, 0))],
        out_specs=pl.BlockSpec((bm, d), lambda i: (i, 0)),
        out_shape=jax.ShapeDtypeStruct((m, d), BF16),
        compiler_params=pltpu.CompilerParams(
            dimension_semantics=("parallel",),
            vmem_limit_bytes=_vmem_limit(2 * bm * d * 6)),
        name=name,
    )(x, w.reshape(1, d).astype(F32))


def _mm_kernel(x_ref, w_ref, o_ref):
    o_ref[...] = _dot(x_ref[...], w_ref[...]).astype(o_ref.dtype)


def _mm_res_kernel(x_ref, w_ref, r_ref, o_ref):
    o_ref[...] = (r_ref[...] + _dot(x_ref[...], w_ref[...])).astype(o_ref.dtype)


def _matmul(x, w, out_dtype, name, residual=None, bm_prefs=(1024, 512, 256, 128, 64, 8),
            bn_prefs=(1024, 512, 256, 128)):
    m, k = x.shape
    _, n = w.shape
    bm = _pick(m, bm_prefs)
    bn = _pick(n, bn_prefs)
    osz = jnp.dtype(out_dtype).itemsize
    nbytes = 2 * (bm * k * 2 + k * bn * 2 + bm * bn * osz)
    in_specs = [pl.BlockSpec((bm, k), lambda i, j: (i, 0)),
                pl.BlockSpec((k, bn), lambda i, j: (0, j))]
    args = [x, w]
    kern = _mm_kernel
    if residual is not None:
        in_specs.append(pl.BlockSpec((bm, bn), lambda i, j: (i, j)))
        args.append(residual)
        kern = _mm_res_kernel
        nbytes += 2 * bm * bn * 4
    return pl.pallas_call(
        kern,
        grid=(m // bm, n // bn),
        in_specs=in_specs,
        out_specs=pl.BlockSpec((bm, bn), lambda i, j: (i, j)),
        out_shape=jax.ShapeDtypeStruct((m, n), out_dtype),
        compiler_params=pltpu.CompilerParams(
            dimension_semantics=("parallel", "parallel"),
            vmem_limit_bytes=_vmem_limit(nbytes)),
        name=name,
    )(*args)


GDN_ROWS = 2 * CHUNK
GDN_HB = 4


def _inv_unit_lower(a_strict, masks, eye):
    t = eye - jnp.where(masks[0], a_strict, 0.0)
    for mk in masks[1:]:
        a_off = jnp.where(mk, a_strict, 0.0)
        t = t - _dot_x3(_dot_x3(t, a_off), t)
    return t


def _gdn_kernel(q_ref, k_ref, v_ref, z_ref, qh_ref, kh_ref, vh_ref,
                cwq_ref, cwk_ref, cwv_ref, scc_ref, scr_ref, hpr_ref, hpc_ref, gnw_ref,
                o_ref, s_ref, ext_ref, *, hb, rows):
    t = pl.program_id(2)

    @pl.when(t == 0)
    def _():
        s_ref[...] = jnp.zeros_like(s_ref)

    keep = (t > 0).astype(F32)

    def conv_silu(x_ref, h_ref, cw_ref):
        ext_ref[0:SUBLANES, :] = h_ref[...].astype(F32) * keep
        ext_ref[SUBLANES:SUBLANES + rows, :] = x_ref[...].astype(F32)
        cw = cw_ref[...]
        y = ext_ref[pl.ds(SUBLANES - 3, rows), :] * cw[0:1, :]
        for i in range(1, CONV_K):
            y = y + ext_ref[pl.ds(SUBLANES - 3 + i, rows), :] * cw[i:i + 1, :]
        return _silu(y)

    q_all = conv_silu(q_ref, qh_ref, cwq_ref)
    k_all = conv_silu(k_ref, kh_ref, cwk_ref)
    v_all = conv_silu(v_ref, vh_ref, cwv_ref)

    scc = scc_ref[...]
    scr = scr_ref[...]
    hpr = hpr_ref[...]
    hpc = hpc_ref[...]
    beta_c = _sigmoid(scc[:, 0:hb])
    g_c = -jnp.exp(hpr[0:1, :]) * _softplus(scc[:, hb:2 * hb] + hpr[1:2, :])
    g_r = -jnp.exp(hpc[:, 0:1]) * _softplus(scr[hb:2 * hb, :] + hpc[:, 1:2])

    ri = lax.broadcasted_iota(jnp.int32, (CHUNK, CHUNK), 0)
    ci = lax.broadcasted_iota(jnp.int32, (CHUNK, CHUNK), 1)
    tril_incl = ri >= ci
    tril_strict = ri > ci
    triu_incl = ri <= ci
    eye = (ri == ci).astype(F32)
    masks = []
    s = 1
    while s < CHUNK:
        masks.append(((ri // (2 * s)) == (ci // (2 * s))) & (((ri // s) % 2) == 1) & (((ci // s) % 2) == 0))
        s *= 2

    gnw = gnw_ref[...]

    for hh in range(hb):
        cs = slice(hh * DH, (hh + 1) * DH)
        s_h = s_ref[hh]
        for c in range(rows // CHUNK):
            rs = slice(c * CHUNK, (c + 1) * CHUNK)
            qc = q_all[rs, cs]
            kc = k_all[rs, cs]
            vc = v_all[rs, cs]
            qc = qc * lax.rsqrt(jnp.sum(qc * qc, axis=-1, keepdims=True) + EPS) * (DH ** -0.5)
            kc = kc * lax.rsqrt(jnp.sum(kc * kc, axis=-1, keepdims=True) + EPS)
            beta = beta_c[rs, hh:hh + 1]
            gcol_raw = g_c[rs, hh:hh + 1]
            grow_raw = g_r[hh:hh + 1, rs]
            gcol = jnp.sum(jnp.where(tril_incl, grow_raw, 0.0), axis=1, keepdims=True)
            grow = jnp.sum(jnp.where(triu_incl, gcol_raw, 0.0), axis=0, keepdims=True)
            decay = jnp.exp(jnp.where(tril_incl, gcol - grow, -jnp.inf))

            kb = kc.astype(BF16)
            qkk = _dot_nt(jnp.concatenate([qc.astype(BF16), kb], axis=0), kb)
            qk = qkk[0:CHUNK] * decay
            a_strict = jnp.where(tril_strict, qkk[CHUNK:2 * CHUNK] * decay * beta, 0.0)
            tinv = _inv_unit_lower(a_strict, masks, eye)
            egc = jnp.exp(gcol)
            rhs = jnp.concatenate([vc * beta, kc * (beta * egc)], axis=1)
            uw = _dot_x3(tinv, rhs)
            u = uw[:, 0:DH]
            w = uw[:, DH:2 * DH]

            sb = s_h.astype(BF16)
            ws = _dot(jnp.concatenate([w, qc * egc], axis=0).astype(BF16), sb)
            v_new = u - ws[0:CHUNK]
            vnb = v_new.astype(BF16)
            o = ws[CHUNK:2 * CHUNK] + _dot(qk.astype(BF16), vnb)
            g_last = gcol[CHUNK - 1:CHUNK, :]
            kd = kc * jnp.exp(g_last - gcol)
            s_h = s_h * jnp.exp(g_last) + _dot_tn(kd.astype(BF16), vnb)

            on = o * lax.rsqrt(jnp.mean(o * o, axis=-1, keepdims=True) + EPS) * gnw
            zc = z_ref[rs, cs].astype(F32)
            o_ref[rs, cs] = (on * _silu(zc)).astype(o_ref.dtype)
        s_ref[hh] = s_h


def _gdn(p3, small, conv_w, a_log, dt_bias, gdn_norm_w):
    b, t, _ = p3.shape
    hb, rows = GDN_HB, GDN_ROWS
    g = HEADS // hb
    hbw = hb * DH
    nblk = HW // hbw
    bl = small[..., 0:HEADS].reshape(b, t, g, hb)
    al = small[..., HEADS:2 * HEADS].reshape(b, t, g, hb)
    scc = jnp.transpose(jnp.concatenate([bl, al], axis=-1), (0, 2, 1, 3))
    scr = jnp.transpose(scc, (0, 1, 3, 2))
    hpr = jnp.stack([a_log.reshape(g, hb), dt_bias.reshape(g, hb)], axis=1).astype(F32)
    hpc = jnp.transpose(hpr, (0, 2, 1))
    cw = conv_w.astype(F32)

    def col(off):
        return lambda bi, gi, ti: (bi, ti, off * nblk + gi)

    def halo(off):
        return lambda bi, gi, ti: (bi, jnp.maximum(ti * (rows // SUBLANES) - 1, 0), off * nblk + gi)

    def cwmap(off):
        return lambda bi, gi, ti: (0, off * nblk + gi)

    blk = lambda off: pl.BlockSpec((None, rows, hbw), col(off))
    hblk = lambda off: pl.BlockSpec((None, SUBLANES, hbw), halo(off))
    in_specs = [blk(0), blk(1), blk(2), blk(3), hblk(0), hblk(1), hblk(2),
                pl.BlockSpec((CONV_K, hbw), cwmap(0)),
                pl.BlockSpec((CONV_K, hbw), cwmap(1)),
                pl.BlockSpec((CONV_K, hbw), cwmap(2)),
                pl.BlockSpec((None, None, rows, 2 * hb), lambda bi, gi, ti: (bi, gi, ti, 0)),
                pl.BlockSpec((None, None, 2 * hb, rows), lambda bi, gi, ti: (bi, gi, 0, ti)),
                pl.BlockSpec((None, 2, hb), lambda bi, gi, ti: (gi, 0, 0)),
                pl.BlockSpec((None, hb, 2), lambda bi, gi, ti: (gi, 0, 0)),
                pl.BlockSpec((1, DH), lambda bi, gi, ti: (0, 0))]
    return pl.pallas_call(
        functools.partial(_gdn_kernel, hb=hb, rows=rows),
        grid=(b, g, t // rows),
        in_specs=in_specs,
        out_specs=pl.BlockSpec((None, rows, hbw), lambda bi, gi, ti: (bi, ti, gi)),
        out_shape=jax.ShapeDtypeStruct((b, t, HW), BF16),
        scratch_shapes=[pltpu.VMEM((hb, DH, DH), F32),
                        pltpu.VMEM((rows + SUBLANES, hbw), F32)],
        compiler_params=pltpu.CompilerParams(
            dimension_semantics=("parallel", "parallel", "arbitrary"),
            vmem_limit_bytes=_vmem_limit(8 << 20)),
        name="gdn",
    )(p3, p3, p3, p3, p3, p3, p3, cw, cw, cw, scc, scr, hpr, hpc,
      gdn_norm_w.reshape(1, DH).astype(F32))


FOX_PREP_ROWS = 512


def _fox_prep_kernel(q_ref, k_ref, f_ref, bf_ref, wq_ref, wk_ref,
                     qn_ref, kn_ref, c_ref, carry_ref, *, rows):
    t = pl.program_id(1)

    @pl.when(t == 0)
    def _():
        carry_ref[...] = jnp.zeros_like(carry_ref)

    wq = wq_ref[...] * (DH ** -0.5)
    wk = wk_ref[...]
    for h in range(HEADS):
        cs = slice(h * DH, (h + 1) * DH)
        q = q_ref[:, cs].astype(F32)
        k = k_ref[:, cs].astype(F32)
        qn_ref[:, cs] = (q * lax.rsqrt(jnp.mean(q * q, axis=-1, keepdims=True) + EPS) * wq).astype(qn_ref.dtype)
        kn_ref[:, cs] = (k * lax.rsqrt(jnp.mean(k * k, axis=-1, keepdims=True) + EPS) * wk).astype(kn_ref.dtype)

    x = f_ref[...] + bf_ref[...]
    ls = -_softplus(-x)
    ri = lax.broadcasted_iota(jnp.int32, (rows, rows), 0)
    ci = lax.broadcasted_iota(jnp.int32, (rows, rows), 1)
    triu = (ri <= ci).astype(BF16)
    hi = ls.astype(BF16)
    r1 = ls - hi.astype(F32)
    mid = r1.astype(BF16)
    lo = (r1 - mid.astype(F32)).astype(BF16)
    c = _dot(hi, triu) + (_dot(mid, triu) + _dot(lo, triu)) + carry_ref[...]
    c_ref[...] = c
    carry_ref[...] = c[:, rows - 1:rows]


def _fox_prep(p3, small, fox_b_f, wq, wk):
    b, t, _ = p3.shape
    rows = _pick(t, (FOX_PREP_ROWS, 256, 128))
    f_row = jnp.transpose(small[..., 2 * HEADS:3 * HEADS], (0, 2, 1))
    qoff = 4 * HW // HW
    return pl.pallas_call(
        functools.partial(_fox_prep_kernel, rows=rows),
        grid=(b, t // rows),
        in_specs=[pl.BlockSpec((None, rows, HW), lambda bi, ti: (bi, ti, qoff)),
                  pl.BlockSpec((None, rows, HW), lambda bi, ti: (bi, ti, qoff + 1)),
                  pl.BlockSpec((None, HEADS, rows), lambda bi, ti: (bi, 0, ti)),
                  pl.BlockSpec((HEADS, 1), lambda bi, ti: (0, 0)),
                  pl.BlockSpec((1, DH), lambda bi, ti: (0, 0)),
                  pl.BlockSpec((1, DH), lambda bi, ti: (0, 0))],
        out_specs=[pl.BlockSpec((None, rows, HW), lambda bi, ti: (bi, ti, 0)),
                   pl.BlockSpec((None, rows, HW), lambda bi, ti: (bi, ti, 0)),
                   pl.BlockSpec((None, HEADS, rows), lambda bi, ti: (bi, 0, ti))],
        out_shape=[jax.ShapeDtypeStruct((b, t, HW), BF16),
                   jax.ShapeDtypeStruct((b, t, HW), BF16),
                   jax.ShapeDtypeStruct((b, HEADS, t), F32)],
        scratch_shapes=[pltpu.VMEM((HEADS, 1), F32)],
        compiler_params=pltpu.CompilerParams(
            dimension_semantics=("parallel", "arbitrary"),
            vmem_limit_bytes=_vmem_limit(16 << 20)),
        name="fox_prep",
    )(p3, p3, f_row, fox_b_f.reshape(HEADS, 1).astype(F32),
      wq.reshape(1, DH).astype(F32), wk.reshape(1, DH).astype(F32))


FOX_BQ = 256


def _fox_kernel(q_ref, k_ref, v_ref, cq_ref, ck_ref, o_ref, *, bq):
    qi = pl.program_id(2)
    q = q_ref[...]
    cq = cq_ref[...]

    def scores(j):
        off = pl.multiple_of(j * bq, bq)
        k = k_ref[pl.ds(off, bq), :]
        v = v_ref[pl.ds(off, bq), :]
        s = _dot_nt(q, k) + (cq - ck_ref[j])
        return s, v

    def update(carry, s, v):
        m, l, acc = carry
        m_new = jnp.maximum(m, jnp.max(s, axis=-1, keepdims=True))
        alpha = jnp.exp(m - m_new)
        p = jnp.exp(s - m_new)
        l = alpha * l + jnp.sum(p, axis=-1, keepdims=True)
        acc = alpha * acc + _dot(p.astype(BF16), v)
        return m_new, l, acc

    def body(j, carry):
        s, v = scores(j)
        return update(carry, s, v)

    init = (jnp.full((bq, 1), -jnp.inf, F32), jnp.zeros((bq, 1), F32), jnp.zeros((bq, DH), F32))
    carry = lax.fori_loop(0, qi, body, init)
    s, v = scores(qi)
    ri = lax.broadcasted_iota(jnp.int32, (bq, bq), 0)
    ci = lax.broadcasted_iota(jnp.int32, (bq, bq), 1)
    s = jnp.where(ri >= ci, s, -jnp.inf)
    _, l, acc = update(carry, s, v)
    o_ref[...] = (acc * (1.0 / l)).astype(o_ref.dtype)


def _fox(qn, kn, p3, c_row):
    b, t, _ = qn.shape
    bq = _pick(t, (FOX_BQ, 128))
    nq = t // bq
    c_col = c_row.reshape(b, HEADS, t, 1)
    c_blk = c_row.reshape(b, HEADS, nq, 1, bq)
    voff = 6 * HW // DH
    return pl.pallas_call(
        functools.partial(_fox_kernel, bq=bq),
        grid=(b, HEADS, nq),
        in_specs=[pl.BlockSpec((None, bq, DH), lambda bi, hi, qi: (bi, qi, hi)),
                  pl.BlockSpec((None, t, DH), lambda bi, hi, qi: (bi, 0, hi)),
                  pl.BlockSpec((None, t, DH), lambda bi, hi, qi: (bi, 0, voff + hi)),
                  pl.BlockSpec((None, None, bq, 1), lambda bi, hi, qi: (bi, hi, qi, 0)),
                  pl.BlockSpec((None, None, nq, 1, bq), lambda bi, hi, qi: (bi, hi, 0, 0, 0))],
        out_specs=pl.BlockSpec((None, bq, DH), lambda bi, hi, qi: (bi, qi, hi)),
        out_shape=jax.ShapeDtypeStruct((b, t, HW), BF16),
        compiler_params=pltpu.CompilerParams(
            dimension_semantics=("parallel", "parallel", "arbitrary"),
            vmem_limit_bytes=_vmem_limit(16 << 20)),
        name="fox",
    )(qn, kn, p3, c_col, c_blk)


def _merge_kernel(oa_ref, ob_ref, wa_ref, wb_ref, ga_ref, gb_ref, o_ref):
    ya = _dot(oa_ref[...], wa_ref[...])
    yb = _dot(ob_ref[...], wb_ref[...])
    o_ref[...] = (_sigmoid(ga_ref[...].astype(F32)) * ya
                  + _sigmoid(gb_ref[...].astype(F32)) * yb).astype(o_ref.dtype)


def _merge(oa, ob, wa, wb, p2, gate_off):
    m, k = oa.shape
    n = wa.shape[1]
    bm = _pick(m, (1024, 512, 256, 128, 64, 8))
    bn = _pick(n, (1024, 512, 256, 128))
    ga0 = gate_off // bn
    gb0 = (gate_off + n) // bn
    nbytes = 2 * (2 * bm * k * 2 + 2 * k * bn * 2 + 3 * bm * bn * 2)
    return pl.pallas_call(
        _merge_kernel,
        grid=(m // bm, n // bn),
        in_specs=[pl.BlockSpec((bm, k), lambda i, j: (i, 0)),
                  pl.BlockSpec((bm, k), lambda i, j: (i, 0)),
                  pl.BlockSpec((k, bn), lambda i, j: (0, j)),
                  pl.BlockSpec((k, bn), lambda i, j: (0, j)),
                  pl.BlockSpec((bm, bn), lambda i, j: (i, ga0 + j)),
                  pl.BlockSpec((bm, bn), lambda i, j: (i, gb0 + j))],
        out_specs=pl.BlockSpec((bm, bn), lambda i, j: (i, j)),
        out_shape=jax.ShapeDtypeStruct((m, n), BF16),
        compiler_params=pltpu.CompilerParams(
            dimension_semantics=("parallel", "parallel"),
            vmem_limit_bytes=_vmem_limit(nbytes)),
        name="merge",
    )(oa, ob, wa, wb, p2, p2)


def _ffn1_kernel(x_ref, wg_ref, wu_ref, o_ref):
    x = x_ref[...]
    g = _dot(x, wg_ref[...])
    u = _dot(x, wu_ref[...])
    o_ref[...] = (_silu(g) * u).astype(o_ref.dtype)


def _ffn1(hn, wg, wu):
    m, k = hn.shape
    n = wg.shape[1]
    bm = _pick(m, (1024, 512, 256, 128, 64, 8))
    bn = _pick(n, (512, 256, 128))
    nbytes = 2 * (bm * k * 2 + 2 * k * bn * 2 + bm * bn * 2)
    return pl.pallas_call(
        _ffn1_kernel,
        grid=(m // bm, n // bn),
        in_specs=[pl.BlockSpec((bm, k), lambda i, j: (i, 0)),
                  pl.BlockSpec((k, bn), lambda i, j: (0, j)),
                  pl.BlockSpec((k, bn), lambda i, j: (0, j))],
        out_specs=pl.BlockSpec((bm, bn), lambda i, j: (i, j)),
        out_shape=jax.ShapeDtypeStruct((m, n), BF16),
        compiler_params=pltpu.CompilerParams(
            dimension_semantics=("parallel", "parallel"),
            vmem_limit_bytes=_vmem_limit(nbytes)),
        name="ffn_up",
    )(hn, wg, wu)


def _ffn2_kernel(a_ref, w_ref, r_ref, o_ref, acc_ref):
    kk = pl.program_id(2)

    @pl.when(kk == 0)
    def _():
        acc_ref[...] = r_ref[...]

    acc_ref[...] += _dot(a_ref[...], w_ref[...])

    @pl.when(kk == pl.num_programs(2) - 1)
    def _():
        o_ref[...] = acc_ref[...]


def _ffn2(act, wd, h):
    m, k = act.shape
    n = wd.shape[1]
    bm = _pick(m, (1024, 512, 256, 128, 64, 8))
    bn = _pick(n, (1024, 512, 256, 128))
    bk = _pick(k, (2816, 2048, 1024, 512, 256, 128))
    nbytes = 2 * (bm * bk * 2 + bk * bn * 2 + 2 * bm * bn * 4) + bm * bn * 4
    return pl.pallas_call(
        _ffn2_kernel,
        grid=(m // bm, n // bn, k // bk),
        in_specs=[pl.BlockSpec((bm, bk), lambda i, j, kk: (i, kk)),
                  pl.BlockSpec((bk, bn), lambda i, j, kk: (kk, j)),
                  pl.BlockSpec((bm, bn), lambda i, j, kk: (i, j))],
        out_specs=pl.BlockSpec((bm, bn), lambda i, j, kk: (i, j)),
        out_shape=jax.ShapeDtypeStruct((m, n), F32),
        scratch_shapes=[pltpu.VMEM((bm, bn), F32)],
        compiler_params=pltpu.CompilerParams(
            dimension_semantics=("parallel", "parallel", "arbitrary"),
            vmem_limit_bytes=_vmem_limit(nbytes)),
        name="ffn_down",
    )(act, wd, h)


FF_PAD = 1024


def _layer(x, norm_mix_w, w_in, conv_w, a_log, dt_bias, gdn_norm_w, fox_b_f,
           fox_q_norm_w, fox_k_norm_w, w_branch_a, w_branch_b, w_out,
           norm_ffn_w, w_ffn_gate, w_ffn_up, w_ffn_down):
    b, t, d = x.shape
    m = b * t
    x2 = x.reshape(m, d)

    o_small_a = 4 * HW
    o_b = o_small_a + 2 * HEADS
    o_f = o_b + 3 * HW
    o_gate = o_f + HEADS
    w_big = jnp.concatenate([w_in[:, 0:o_small_a], w_in[:, o_b:o_f], w_in[:, o_gate:]], axis=1).astype(BF16)
    w_small = jnp.concatenate([w_in[:, o_small_a:o_b], w_in[:, o_f:o_gate]], axis=1)
    w_small = jnp.pad(w_small, ((0, 0), (0, LANES - 3 * HEADS))).astype(BF16)
    gate_off = 7 * HW

    xn = _rmsnorm(x2, norm_mix_w, "norm_mix")
    p2 = _matmul(xn, w_big, BF16, "in_proj")
    small = _matmul(xn, w_small, F32, "in_proj_small")
    p3 = p2.reshape(b, t, p2.shape[1])
    small3 = small.reshape(b, t, LANES)

    o_a = _gdn(p3, small3, conv_w, a_log, dt_bias, gdn_norm_w)
    qn, kn, c_row = _fox_prep(p3, small3, fox_b_f, fox_q_norm_w, fox_k_norm_w)
    o_b_ = _fox(qn, kn, p3, c_row)

    merged = _merge(o_a.reshape(m, HW), o_b_.reshape(m, HW),
                    w_branch_a.astype(BF16), w_branch_b.astype(BF16), p2, gate_off)
    h = _matmul(merged, w_out.astype(BF16), F32, "out_proj", residual=x2)

    hn = _rmsnorm(h, norm_ffn_w, "norm_ffn")
    dff = w_ffn_gate.shape[1]
    pad = (-dff) % FF_PAD
    wg = jnp.pad(w_ffn_gate, ((0, 0), (0, pad))).astype(BF16)
    wu = jnp.pad(w_ffn_up, ((0, 0), (0, pad))).astype(BF16)
    wd = jnp.pad(w_ffn_down, ((0, pad), (0, 0))).astype(BF16)
    act = _ffn1(hn, wg, wu)
    out = _ffn2(act, wd, h)
    return out.reshape(b, t, d)


def kernel(x, norm_mix_w, w_in, conv_w, a_log, dt_bias, gdn_norm_w, fox_b_f, fox_q_norm_w, fox_k_norm_w, w_branch_a, w_branch_b, w_out, norm_ffn_w, w_ffn_gate, w_ffn_up, w_ffn_down):
    h = x
    for l in range(norm_mix_w.shape[0]):
        h = _layer(h, norm_mix_w[l], w_in[l], conv_w[l], a_log[l], dt_bias[l], gdn_norm_w[l],
                   fox_b_f[l], fox_q_norm_w[l], fox_k_norm_w[l], w_branch_a[l], w_branch_b[l],
                   w_out[l], norm_ffn_w[l], w_ffn_gate[l], w_ffn_up[l], w_ffn_down[l])
    return h
```

```python
import functools

import jax
import jax.numpy as jnp
from jax import lax
from jax.experimental import pallas as pl
from jax.experimental.pallas import tpu as pltpu

F32 = jnp.float32
BF16 = jnp.bfloat16

EPS = 1e-6
HEADS = 16
DH = 128
HW = HEADS * DH
CONV_K = 4

V7X_VMEM_BYTES = 64 * 1024 * 1024
LANES = 128
SUBLANES = 8
LOG2E = 1.4426950408889634


def _vmem_limit(nbytes):
    return int(min(nbytes + (16 << 20), V7X_VMEM_BYTES - (4 << 20)))


def _pick(n, prefs):
    for p in prefs:
        if n % p == 0:
            return p
    return n


def _sigmoid(x):
    return 1.0 / (1.0 + jnp.exp(-x))


def _silu(x):
    return x * _sigmoid(x)


def _softplus(x):
    return jnp.maximum(x, 0.0) + jnp.log(1.0 + jnp.exp(-jnp.abs(x)))


def _dot(a, b):
    return jnp.dot(a, b, preferred_element_type=F32)


def _dot_nt(a, b):
    return lax.dot_general(a, b, (((1,), (1,)), ((), ())), preferred_element_type=F32)


def _dot_tn(a, b):
    return lax.dot_general(a, b, (((0,), (0,)), ((), ())), preferred_element_type=F32)


def _rmsnorm_kernel(x_ref, w_ref, o_ref):
    x = x_ref[...]
    ms = jnp.mean(x * x, axis=-1, keepdims=True)
    o_ref[...] = (x * lax.rsqrt(ms + EPS) * w_ref[...]).astype(o_ref.dtype)


def _rmsnorm(x, w, name):
    m, d = x.shape
    bm = _pick(m, (256, 128, 64, 8))
    return pl.pallas_call(
        _rmsnorm_kernel,
        grid=(m // bm,),
        in_specs=[pl.BlockSpec((bm, d), lambda i: (i, 0)),
                  pl.BlockSpec((1, d), lambda i: (0, 0))],
        out_specs=pl.BlockSpec((bm, d), lambda i: (i, 0)),
        out_shape=jax.ShapeDtypeStruct((m, d), BF16),
        compiler_params=pltpu.CompilerParams(
            dimension_semantics=("parallel",),
            vmem_limit_bytes=_vmem_limit(2 * bm * d * 6)),
        name=name,
    )(x, w.reshape(1, d).astype(F32))


def _mm_kernel(x_ref, w_ref, o_ref):
    o_ref[...] = _dot(x_ref[...], w_ref[...]).astype(o_ref.dtype)


def _mm_res_kernel(x_ref, w_ref, r_ref, o_ref):
    o_ref[...] = (r_ref[...] + _dot(x_ref[...], w_ref[...])).astype(o_ref.dtype)


def _matmul(x, w, out_dtype, name, residual=None, bm_prefs=(1024, 512, 256, 128, 64, 8),
            bn_prefs=(1024, 512, 256, 128)):
    m, k = x.shape
    _, n = w.shape
    bm = _pick(m, bm_prefs)
    bn = _pick(n, bn_prefs)
    osz = jnp.dtype(out_dtype).itemsize
    nbytes = 2 * (bm * k * 2 + k * bn * 2 + bm * bn * osz)
    in_specs = [pl.BlockSpec((bm, k), lambda i, j: (i, 0)),
                pl.BlockSpec((k, bn), lambda i, j: (0, j))]
    args = [x, w]
    kern = _mm_kernel
    if residual is not None:
        in_specs.append(pl.BlockSpec((bm, bn), lambda i, j: (i, j)))
        args.append(residual)
        kern = _mm_res_kernel
        nbytes += 2 * bm * bn * 4
    return pl.pallas_call(
        kern,
        grid=(m // bm, n // bn),
        in_specs=in_specs,
        out_specs=pl.BlockSpec((bm, bn), lambda i, j: (i, j)),
        out_shape=jax.ShapeDtypeStruct((m, n), out_dtype),
        compiler_params=pltpu.CompilerParams(
            dimension_semantics=("parallel", "parallel"),
            vmem_limit_bytes=_vmem_limit(nbytes)),
        name=name,
    )(*args)


GDN_CH = 128
GDN_HB = 8


def _gdn_kernel(q_ref, k_ref, v_ref, z_ref, qh_ref, kh_ref, vh_ref,
                cwq_ref, cwk_ref, cwv_ref, scc_ref, scr_ref, hpr_ref, hpc_ref, gnw_ref,
                o_ref, s_ref, ext_ref, *, hb, rows):
    t = pl.program_id(2)

    @pl.when(t == 0)
    def _():
        s_ref[...] = jnp.zeros_like(s_ref)

    keep = (t > 0).astype(F32)

    def conv_silu(x_ref, h_ref, cw_ref):
        ext_ref[0:SUBLANES, :] = h_ref[...].astype(F32) * keep
        ext_ref[SUBLANES:SUBLANES + rows, :] = x_ref[...].astype(F32)
        cw = cw_ref[...]
        y = ext_ref[pl.ds(SUBLANES - 3, rows), :] * cw[0:1, :]
        for i in range(1, CONV_K):
            y = y + ext_ref[pl.ds(SUBLANES - 3 + i, rows), :] * cw[i:i + 1, :]
        return _silu(y)

    q_all = conv_silu(q_ref, qh_ref, cwq_ref)
    k_all = conv_silu(k_ref, kh_ref, cwk_ref)
    v_all = conv_silu(v_ref, vh_ref, cwv_ref)

    scc = scc_ref[...]
    scr = scr_ref[...]
    hpr = hpr_ref[...]
    hpc = hpc_ref[...]
    beta_c = _sigmoid(scc[:, 0:hb])
    g_c = -jnp.exp(hpr[0:1, :]) * _softplus(scc[:, hb:2 * hb] + hpr[1:2, :])
    g_r = -jnp.exp(hpc[:, 0:1]) * _softplus(scr[hb:2 * hb, :] + hpc[:, 1:2])

    ri = lax.broadcasted_iota(jnp.int32, (rows, rows), 0)
    ci = lax.broadcasted_iota(jnp.int32, (rows, rows), 1)
    tril_incl = ri >= ci
    tril_strict = ri > ci
    triu_incl = ri <= ci
    eye = (ri == ci).astype(F32)
    masks = []
    s = 1
    while s < rows:
        masks.append(((ri // (2 * s)) == (ci // (2 * s))) & (((ri // s) % 2) == 1) & (((ci // s) % 2) == 0))
        s *= 2
    gnw = gnw_ref[...]

    pr = []
    for hh in range(hb):
        cs = slice(hh * DH, (hh + 1) * DH)
        qc = q_all[:, cs]
        kc = k_all[:, cs]
        vc = v_all[:, cs]
        qc = qc * (lax.rsqrt(jnp.sum(qc * qc, axis=-1, keepdims=True) + EPS) * (DH ** -0.5))
        kc = kc * lax.rsqrt(jnp.sum(kc * kc, axis=-1, keepdims=True) + EPS)
        beta = beta_c[:, hh:hh + 1]
        gcol = jnp.sum(jnp.where(tril_incl, g_r[hh:hh + 1, :], 0.0), axis=1, keepdims=True)
        grow = jnp.sum(jnp.where(triu_incl, g_c[:, hh:hh + 1], 0.0), axis=0, keepdims=True)
        decay = jnp.exp(jnp.where(tril_incl, gcol - grow, -jnp.inf))
        kb = kc.astype(BF16)
        qkk = _dot_nt(jnp.concatenate([qc.astype(BF16), kb], axis=0), kb)
        qk = (qkk[0:rows] * decay).astype(BF16)
        a_strict = jnp.where(tril_strict, qkk[rows:2 * rows] * (decay * beta), 0.0)
        egc = jnp.exp(gcol)
        g_last = gcol[rows - 1:rows, :]
        pr.append(dict(
            a=a_strict.astype(BF16),
            t=eye - jnp.where(masks[0], a_strict, 0.0),
            rhs=jnp.concatenate([vc * beta, kc * (beta * egc)], axis=1).astype(BF16),
            qe=(qc * egc).astype(BF16),
            kd=(kc * jnp.exp(g_last - gcol)).astype(BF16),
            qk=qk, gl=jnp.exp(g_last)))

    for mk in masks[1:]:
        for p in pr:
            tb = p["t"].astype(BF16)
            y = _dot(_dot(tb, p["a"]).astype(BF16), tb)
            p["t"] = p["t"] - jnp.where(mk, y, 0.0)

    for p in pr:
        p["uw"] = _dot(p["t"].astype(BF16), p["rhs"])

    for hh, p in enumerate(pr):
        cs = slice(hh * DH, (hh + 1) * DH)
        s_h = s_ref[hh]
        uw = p["uw"]
        ws = _dot(jnp.concatenate([uw[:, DH:2 * DH].astype(BF16), p["qe"]], axis=0), s_h.astype(BF16))
        vnb = (uw[:, 0:DH] - ws[0:rows]).astype(BF16)
        o = ws[rows:2 * rows] + _dot(p["qk"], vnb)
        s_ref[hh] = s_h * p["gl"] + _dot_tn(p["kd"], vnb)
        on = o * lax.rsqrt(jnp.mean(o * o, axis=-1, keepdims=True) + EPS) * gnw
        zc = z_ref[:, cs].astype(F32)
        o_ref[:, cs] = (on * _silu(zc)).astype(o_ref.dtype)


def _gdn(p3, small, conv_w, a_log, dt_bias, gdn_norm_w):
    b, t, _ = p3.shape
    hb, rows = GDN_HB, GDN_CH
    g = HEADS // hb
    hbw = hb * DH
    nblk = HW // hbw
    bl = small[..., 0:HEADS].reshape(b, t, g, hb)
    al = small[..., HEADS:2 * HEADS].reshape(b, t, g, hb)
    scc = jnp.transpose(jnp.concatenate([bl, al], axis=-1), (0, 2, 1, 3))
    scr = jnp.transpose(scc, (0, 1, 3, 2))
    hpr = jnp.stack([a_log.reshape(g, hb), dt_bias.reshape(g, hb)], axis=1).astype(F32)
    hpc = jnp.transpose(hpr, (0, 2, 1))
    cw = conv_w.astype(F32)

    def col(off):
        return lambda bi, gi, ti: (bi, ti, off * nblk + gi)

    def halo(off):
        return lambda bi, gi, ti: (bi, jnp.maximum(ti * (rows // SUBLANES) - 1, 0), off * nblk + gi)

    def cwmap(off):
        return lambda bi, gi, ti: (0, off * nblk + gi)

    blk = lambda off: pl.BlockSpec((None, rows, hbw), col(off))
    hblk = lambda off: pl.BlockSpec((None, SUBLANES, hbw), halo(off))
    in_specs = [blk(0), blk(1), blk(2), blk(3), hblk(0), hblk(1), hblk(2),
                pl.BlockSpec((CONV_K, hbw), cwmap(0)),
                pl.BlockSpec((CONV_K, hbw), cwmap(1)),
                pl.BlockSpec((CONV_K, hbw), cwmap(2)),
                pl.BlockSpec((None, None, rows, 2 * hb), lambda bi, gi, ti: (bi, gi, ti, 0)),
                pl.BlockSpec((None, None, 2 * hb, rows), lambda bi, gi, ti: (bi, gi, 0, ti)),
                pl.BlockSpec((None, 2, hb), lambda bi, gi, ti: (gi, 0, 0)),
                pl.BlockSpec((None, hb, 2), lambda bi, gi, ti: (gi, 0, 0)),
                pl.BlockSpec((1, DH), lambda bi, gi, ti: (0, 0))]
    return pl.pallas_call(
        functools.partial(_gdn_kernel, hb=hb, rows=rows),
        grid=(b, g, t // rows),
        in_specs=in_specs,
        out_specs=pl.BlockSpec((None, rows, hbw), lambda bi, gi, ti: (bi, ti, gi)),
        out_shape=jax.ShapeDtypeStruct((b, t, HW), BF16),
        scratch_shapes=[pltpu.VMEM((hb, DH, DH), F32),
                        pltpu.VMEM((rows + SUBLANES, hbw), F32)],
        compiler_params=pltpu.CompilerParams(
            dimension_semantics=("parallel", "parallel", "arbitrary"),
            vmem_limit_bytes=_vmem_limit(8 << 20)),
        name="gdn",
    )(p3, p3, p3, p3, p3, p3, p3, cw, cw, cw, scc, scr, hpr, hpc,
      gdn_norm_w.reshape(1, DH).astype(F32))


FOX_PREP_ROWS = 512


def _fox_prep_kernel(q_ref, k_ref, f_ref, bf_ref, wq_ref, wk_ref,
                     qn_ref, kn_ref, c_ref, carry_ref, *, rows):
    t = pl.program_id(1)

    @pl.when(t == 0)
    def _():
        carry_ref[...] = jnp.zeros_like(carry_ref)

    wq = wq_ref[...] * (DH ** -0.5 * LOG2E)
    wk = wk_ref[...]
    for h in range(HEADS):
        cs = slice(h * DH, (h + 1) * DH)
        q = q_ref[:, cs].astype(F32)
        k = k_ref[:, cs].astype(F32)
        qn_ref[:, cs] = (q * lax.rsqrt(jnp.mean(q * q, axis=-1, keepdims=True) + EPS) * wq).astype(qn_ref.dtype)
        kn_ref[:, cs] = (k * lax.rsqrt(jnp.mean(k * k, axis=-1, keepdims=True) + EPS) * wk).astype(kn_ref.dtype)

    x = f_ref[...] + bf_ref[...]
    ls = -_softplus(-x)
    ri = lax.broadcasted_iota(jnp.int32, (rows, rows), 0)
    ci = lax.broadcasted_iota(jnp.int32, (rows, rows), 1)
    triu = (ri <= ci).astype(BF16)
    hi = ls.astype(BF16)
    r1 = ls - hi.astype(F32)
    mid = r1.astype(BF16)
    lo = (r1 - mid.astype(F32)).astype(BF16)
    c = _dot(hi, triu) + (_dot(mid, triu) + _dot(lo, triu)) + carry_ref[...]
    c_ref[...] = c * LOG2E
    carry_ref[...] = c[:, rows - 1:rows]


def _fox_prep(p3, small, fox_b_f, wq, wk):
    b, t, _ = p3.shape
    rows = _pick(t, (FOX_PREP_ROWS, 256, 128))
    f_row = jnp.transpose(small[..., 2 * HEADS:3 * HEADS], (0, 2, 1))
    qoff = 4
    return pl.pallas_call(
        functools.partial(_fox_prep_kernel, rows=rows),
        grid=(b, t // rows),
        in_specs=[pl.BlockSpec((None, rows, HW), lambda bi, ti: (bi, ti, qoff)),
                  pl.BlockSpec((None, rows, HW), lambda bi, ti: (bi, ti, qoff + 1)),
                  pl.BlockSpec((None, HEADS, rows), lambda bi, ti: (bi, 0, ti)),
                  pl.BlockSpec((HEADS, 1), lambda bi, ti: (0, 0)),
                  pl.BlockSpec((1, DH), lambda bi, ti: (0, 0)),
                  pl.BlockSpec((1, DH), lambda bi, ti: (0, 0))],
        out_specs=[pl.BlockSpec((None, rows, HW), lambda bi, ti: (bi, ti, 0)),
                   pl.BlockSpec((None, rows, HW), lambda bi, ti: (bi, ti, 0)),
                   pl.BlockSpec((None, HEADS, rows), lambda bi, ti: (bi, 0, ti))],
        out_shape=[jax.ShapeDtypeStruct((b, t, HW), BF16),
                   jax.ShapeDtypeStruct((b, t, HW), BF16),
                   jax.ShapeDtypeStruct((b, HEADS, t), F32)],
        scratch_shapes=[pltpu.VMEM((HEADS, 1), F32)],
        compiler_params=pltpu.CompilerParams(
            dimension_semantics=("parallel", "arbitrary"),
            vmem_limit_bytes=_vmem_limit(16 << 20)),
        name="fox_prep",
    )(p3, p3, f_row, fox_b_f.reshape(HEADS, 1).astype(F32),
      wq.reshape(1, DH).astype(F32), wk.reshape(1, DH).astype(F32))


FOX_BQ = 512
FOX_SUB = 256


def _fox_kernel(q_ref, k_ref, v_ref, ck_ref, o_ref, *, bq, sub):
    qi = pl.program_id(2)
    nsub = bq // sub
    qs = [q_ref[i * sub:(i + 1) * sub, :] for i in range(nsub)]

    def chain(carry, q, k, v, ck, mask):
        m, l, acc = carry
        s = _dot_nt(q, k) - ck
        if mask is not None:
            s = jnp.where(mask, s, -jnp.inf)
        m_new = jnp.maximum(m, jnp.max(s, axis=-1, keepdims=True))
        alpha = jnp.exp2(m - m_new)
        p = jnp.exp2(s - m_new)
        l = alpha * l + jnp.sum(p, axis=-1, keepdims=True)
        acc = alpha * acc + _dot(p.astype(BF16), v)
        return m_new, l, acc

    def body(j, carry):
        off = pl.multiple_of(j * bq, bq)
        k = k_ref[pl.ds(off, bq), :]
        v = v_ref[pl.ds(off, bq), :]
        ck = ck_ref[j]
        return tuple(chain(carry[i], qs[i], k, v, ck, None) for i in range(nsub))

    init = tuple((jnp.full((sub, 1), -jnp.inf, F32), jnp.zeros((sub, 1), F32), jnp.zeros((sub, DH), F32))
                 for _ in range(nsub))
    carry = lax.fori_loop(0, qi, body, init)

    off = pl.multiple_of(qi * bq, bq)
    ckd = ck_ref[qi]
    for i in range(nsub):
        w = (i + 1) * sub
        k = k_ref[pl.ds(off, w), :]
        v = v_ref[pl.ds(off, w), :]
        ri = lax.broadcasted_iota(jnp.int32, (sub, w), 0) + i * sub
        ci = lax.broadcasted_iota(jnp.int32, (sub, w), 1)
        _, l, acc = chain(carry[i], qs[i], k, v, ckd[:, 0:w], ri >= ci)
        o_ref[i * sub:(i + 1) * sub, :] = (acc * (1.0 / l)).astype(o_ref.dtype)


def _fox(qn, kn, p3, c_row):
    b, t, _ = qn.shape
    bq = _pick(t, (FOX_BQ, 256, 128))
    sub = min(FOX_SUB, bq)
    nq = t // bq
    c_blk = c_row.reshape(b, HEADS, nq, 1, bq)
    voff = 6 * HW // DH
    return pl.pallas_call(
        functools.partial(_fox_kernel, bq=bq, sub=sub),
        grid=(b, HEADS, nq),
        in_specs=[pl.BlockSpec((None, bq, DH), lambda bi, hi, qi: (bi, qi, hi)),
                  pl.BlockSpec((None, t, DH), lambda bi, hi, qi: (bi, 0, hi)),
                  pl.BlockSpec((None, t, DH), lambda bi, hi, qi: (bi, 0, voff + hi)),
                  pl.BlockSpec((None, None, nq, 1, bq), lambda bi, hi, qi: (bi, hi, 0, 0, 0))],
        out_specs=pl.BlockSpec((None, bq, DH), lambda bi, hi, qi: (bi, qi, hi)),
        out_shape=jax.ShapeDtypeStruct((b, t, HW), BF16),
        compiler_params=pltpu.CompilerParams(
            dimension_semantics=("parallel", "parallel", "arbitrary"),
            vmem_limit_bytes=_vmem_limit(16 << 20)),
        name="fox",
    )(qn, kn, p3, c_blk)


def _merge_kernel(oa_ref, ob_ref, wa_ref, wb_ref, ga_ref, gb_ref, o_ref):
    ya = _dot(oa_ref[...], wa_ref[...])
    yb = _dot(ob_ref[...], wb_ref[...])
    o_ref[...] = (_sigmoid(ga_ref[...].astype(F32)) * ya
                  + _sigmoid(gb_ref[...].astype(F32)) * yb).astype(o_ref.dtype)


def _merge(oa, ob, wa, wb, p2, gate_off):
    m, k = oa.shape
    n = wa.shape[1]
    bm = _pick(m, (1024, 512, 256, 128, 64, 8))
    bn = _pick(n, (1024, 512, 256, 128))
    ga0 = gate_off // bn
    gb0 = (gate_off + n) // bn
    nbytes = 2 * (2 * bm * k * 2 + 2 * k * bn * 2 + 3 * bm * bn * 2)
    return pl.pallas_call(
        _merge_kernel,
        grid=(m // bm, n // bn),
        in_specs=[pl.BlockSpec((bm, k), lambda i, j: (i, 0)),
                  pl.BlockSpec((bm, k), lambda i, j: (i, 0)),
                  pl.BlockSpec((k, bn), lambda i, j: (0, j)),
                  pl.BlockSpec((k, bn), lambda i, j: (0, j)),
                  pl.BlockSpec((bm, bn), lambda i, j: (i, ga0 + j)),
                  pl.BlockSpec((bm, bn), lambda i, j: (i, gb0 + j))],
        out_specs=pl.BlockSpec((bm, bn), lambda i, j: (i, j)),
        out_shape=jax.ShapeDtypeStruct((m, n), BF16),
        compiler_params=pltpu.CompilerParams(
            dimension_semantics=("parallel", "parallel"),
            vmem_limit_bytes=_vmem_limit(nbytes)),
        name="merge",
    )(oa, ob, wa, wb, p2, p2)


def _ffn1_kernel(x_ref, wg_ref, wu_ref, o_ref):
    x = x_ref[...]
    g = _dot(x, wg_ref[...])
    u = _dot(x, wu_ref[...])
    o_ref[...] = (_silu(g) * u).astype(o_ref.dtype)


def _ffn1(hn, wg, wu):
    m, k = hn.shape
    n = wg.shape[1]
    bm = _pick(m, (1024, 512, 256, 128, 64, 8))
    bn = _pick(n, (512, 256, 128))
    nbytes = 2 * (bm * k * 2 + 2 * k * bn * 2 + bm * bn * 2)
    return pl.pallas_call(
        _ffn1_kernel,
        grid=(m // bm, n // bn),
        in_specs=[pl.BlockSpec((bm, k), lambda i, j: (i, 0)),
                  pl.BlockSpec((k, bn), lambda i, j: (0, j)),
                  pl.BlockSpec((k, bn), lambda i, j: (0, j))],
        out_specs=pl.BlockSpec((bm, bn), lambda i, j: (i, j)),
        out_shape=jax.ShapeDtypeStruct((m, n), BF16),
        compiler_params=pltpu.CompilerParams(
            dimension_semantics=("parallel", "parallel"),
            vmem_limit_bytes=_vmem_limit(nbytes)),
        name="ffn_up",
    )(hn, wg, wu)


def _ffn2_kernel(a_ref, w_ref, r_ref, o_ref, acc_ref):
    kk = pl.program_id(2)

    @pl.when(kk == 0)
    def _():
        acc_ref[...] = r_ref[...]

    acc_ref[...] += _dot(a_ref[...], w_ref[...])

    @pl.when(kk == pl.num_programs(2) - 1)
    def _():
        o_ref[...] = acc_ref[...]


def _ffn2(act, wd, h):
    m, k = act.shape
    n = wd.shape[1]
    bm = _pick(m, (1024, 512, 256, 128, 64, 8))
    bn = _pick(n, (1024, 512, 256, 128))
    bk = _pick(k, (2816, 2048, 1024, 512, 256, 128))
    nbytes = 2 * (bm * bk * 2 + bk * bn * 2 + 2 * bm * bn * 4) + bm * bn * 4
    return pl.pallas_call(
        _ffn2_kernel,
        grid=(m // bm, n // bn, k // bk),
        in_specs=[pl.BlockSpec((bm, bk), lambda i, j, kk: (i, kk)),
                  pl.BlockSpec((bk, bn), lambda i, j, kk: (kk, j)),
                  pl.BlockSpec((bm, bn), lambda i, j, kk: (i, j))],
        out_specs=pl.BlockSpec((bm, bn), lambda i, j, kk: (i, j)),
        out_shape=jax.ShapeDtypeStruct((m, n), F32),
        scratch_shapes=[pltpu.VMEM((bm, bn), F32)],
        compiler_params=pltpu.CompilerParams(
            dimension_semantics=("parallel", "parallel", "arbitrary"),
            vmem_limit_bytes=_vmem_limit(nbytes)),
        name="ffn_down",
    )(act, wd, h)


FF_PAD = 1024


def _layer(x, norm_mix_w, w_in, conv_w, a_log, dt_bias, gdn_norm_w, fox_b_f,
           fox_q_norm_w, fox_k_norm_w, w_branch_a, w_branch_b, w_out,
           norm_ffn_w, w_ffn_gate, w_ffn_up, w_ffn_down):
    b, t, d = x.shape
    m = b * t
    x2 = x.reshape(m, d)

    o_small_a = 4 * HW
    o_b = o_small_a + 2 * HEADS
    o_f = o_b + 3 * HW
    o_gate = o_f + HEADS
    w_big = jnp.concatenate([w_in[:, 0:o_small_a], w_in[:, o_b:o_f], w_in[:, o_gate:]], axis=1).astype(BF16)
    w_small = jnp.concatenate([w_in[:, o_small_a:o_b], w_in[:, o_f:o_gate]], axis=1)
    w_small = jnp.pad(w_small, ((0, 0), (0, LANES - 3 * HEADS))).astype(BF16)
    gate_off = 7 * HW

    xn = _rmsnorm(x2, norm_mix_w, "norm_mix")
    p2 = _matmul(xn, w_big, BF16, "in_proj")
    small = _matmul(xn, w_small, F32, "in_proj_small")
    p3 = p2.reshape(b, t, p2.shape[1])
    small3 = small.reshape(b, t, LANES)

    o_a = _gdn(p3, small3, conv_w, a_log, dt_bias, gdn_norm_w)
    qn, kn, c_row = _fox_prep(p3, small3, fox_b_f, fox_q_norm_w, fox_k_norm_w)
    o_b_ = _fox(qn, kn, p3, c_row)

    merged = _merge(o_a.reshape(m, HW), o_b_.reshape(m, HW),
                    w_branch_a.astype(BF16), w_branch_b.astype(BF16), p2, gate_off)
    h = _matmul(merged, w_out.astype(BF16), F32, "out_proj", residual=x2)

    hn = _rmsnorm(h, norm_ffn_w, "norm_ffn")
    dff = w_ffn_gate.shape[1]
    pad = (-dff) % FF_PAD
    wg = jnp.pad(w_ffn_gate, ((0, 0), (0, pad))).astype(BF16)
    wu = jnp.pad(w_ffn_up, ((0, 0), (0, pad))).astype(BF16)
    wd = jnp.pad(w_ffn_down, ((0, pad), (0, 0))).astype(BF16)
    act = _ffn1(hn, wg, wu)
    out = _ffn2(act, wd, h)
    return out.reshape(b, t, d)


def kernel(x, norm_mix_w, w_in, conv_w, a_log, dt_bias, gdn_norm_w, fox_b_f, fox_q_norm_w, fox_k_norm_w, w_branch_a, w_branch_b, w_out, norm_ffn_w, w_ffn_gate, w_ffn_up, w_ffn_down):
    h = x
    for l in range(norm_mix_w.shape[0]):
        h = _layer(h, norm_mix_w[l], w_in[l], conv_w[l], a_log[l], dt_bias[l], gdn_norm_w[l],
                   fox_b_f[l], fox_q_norm_w[l], fox_k_norm_w[l], w_branch_a[l], w_branch_b[l],
                   w_out[l], norm_ffn_w[l], w_ffn_gate[l], w_ffn_up[l], w_ffn_down[l])
    return h
```

```python
import functools

import jax
import jax.numpy as jnp
from jax import lax
from jax.experimental import pallas as pl
from jax.experimental.pallas import tpu as pltpu

F32 = jnp.float32
BF16 = jnp.bfloat16

EPS = 1e-6
HEADS = 16
DH = 128
HW = HEADS * DH
CONV_K = 4

V7X_VMEM_BYTES = 64 * 1024 * 1024
LANES = 128
SUBLANES = 8
LOG2E = 1.4426950408889634


def _vmem_limit(nbytes):
    return int(min(nbytes + (16 << 20), V7X_VMEM_BYTES - (4 << 20)))


def _pick(n, prefs):
    for p in prefs:
        if n % p == 0:
            return p
    return n


def _sigmoid(x):
    return 1.0 / (1.0 + jnp.exp(-x))


def _silu(x):
    return x * _sigmoid(x)


def _softplus(x):
    return jnp.maximum(x, 0.0) + jnp.log(1.0 + jnp.exp(-jnp.abs(x)))


def _dot(a, b):
    return jnp.dot(a, b, preferred_element_type=F32)


def _dot_nt(a, b):
    return lax.dot_general(a, b, (((1,), (1,)), ((), ())), preferred_element_type=F32)


def _dot_tn(a, b):
    return lax.dot_general(a, b, (((0,), (0,)), ((), ())), preferred_element_type=F32)


def _rmsnorm_kernel(x_ref, w_ref, o_ref):
    x = x_ref[...]
    ms = jnp.mean(x * x, axis=-1, keepdims=True)
    o_ref[...] = (x * lax.rsqrt(ms + EPS) * w_ref[...]).astype(o_ref.dtype)


def _rmsnorm(x, w, name):
    m, d = x.shape
    bm = _pick(m, (256, 128, 64, 8))
    return pl.pallas_call(
        _rmsnorm_kernel,
        grid=(m // bm,),
        in_specs=[pl.BlockSpec((bm, d), lambda i: (i, 0)),
                  pl.BlockSpec((1, d), lambda i: (0, 0))],
        out_specs=pl.BlockSpec((bm, d), lambda i: (i, 0)),
        out_shape=jax.ShapeDtypeStruct((m, d), BF16),
        compiler_params=pltpu.CompilerParams(
            dimension_semantics=("parallel",),
            vmem_limit_bytes=_vmem_limit(2 * bm * d * 6)),
        name=name,
    )(x, w.reshape(1, d).astype(F32))


def _mm_kernel(x_ref, w_ref, o_ref):
    o_ref[...] = _dot(x_ref[...], w_ref[...]).astype(o_ref.dtype)


def _mm_res_kernel(x_ref, w_ref, r_ref, o_ref):
    o_ref[...] = (r_ref[...] + _dot(x_ref[...], w_ref[...])).astype(o_ref.dtype)


def _matmul(x, w, out_dtype, name, residual=None, bm_prefs=(1024, 512, 256, 128, 64, 8),
            bn_prefs=(1024, 512, 256, 128)):
    m, k = x.shape
    _, n = w.shape
    bm = _pick(m, bm_prefs)
    bn = _pick(n, bn_prefs)
    osz = jnp.dtype(out_dtype).itemsize
    nbytes = 2 * (bm * k * 2 + k * bn * 2 + bm * bn * osz)
    in_specs = [pl.BlockSpec((bm, k), lambda i, j: (i, 0)),
                pl.BlockSpec((k, bn), lambda i, j: (0, j))]
    args = [x, w]
    kern = _mm_kernel
    if residual is not None:
        in_specs.append(pl.BlockSpec((bm, bn), lambda i, j: (i, j)))
        args.append(residual)
        kern = _mm_res_kernel
        nbytes += 2 * bm * bn * 4
    return pl.pallas_call(
        kern,
        grid=(m // bm, n // bn),
        in_specs=in_specs,
        out_specs=pl.BlockSpec((bm, bn), lambda i, j: (i, j)),
        out_shape=jax.ShapeDtypeStruct((m, n), out_dtype),
        compiler_params=pltpu.CompilerParams(
            dimension_semantics=("parallel", "parallel"),
            vmem_limit_bytes=_vmem_limit(nbytes)),
        name=name,
    )(*args)


GDN_CH = 128
GDN_HB = 8
GDN_HALO = 16


def _gdn_kernel(q_ref, k_ref, v_ref, z_ref, qh_ref, kh_ref, vh_ref,
                cwq_ref, cwk_ref, cwv_ref, scc_ref, scr_ref, hpr_ref, hpc_ref, gnw_ref,
                o_ref, s_ref, *, hb, rows):
    t = pl.program_id(2)

    @pl.when(t == 0)
    def _():
        s_ref[...] = jnp.zeros_like(s_ref)

    sr = lax.broadcasted_iota(jnp.int32, (3 * rows, rows + GDN_HALO), 0)
    sc = lax.broadcasted_iota(jnp.int32, (3 * rows, rows + GDN_HALO), 1)
    shift = (sc == (sr % rows) + (sr // rows) + (GDN_HALO - 3)).astype(BF16)

    def conv_silu(x_ref, h_ref, cw_ref):
        x = x_ref[...]
        halo = jnp.where(t > 0, h_ref[...], jnp.zeros_like(h_ref))
        sh = _dot(shift, jnp.concatenate([halo, x], axis=0))
        cw = cw_ref[...]
        y = x.astype(F32) * cw[CONV_K - 1:CONV_K, :]
        for i in range(CONV_K - 1):
            y = y + sh[i * rows:(i + 1) * rows] * cw[i:i + 1, :]
        return _silu(y)

    q_all = conv_silu(q_ref, qh_ref, cwq_ref)
    k_all = conv_silu(k_ref, kh_ref, cwk_ref)
    v_all = conv_silu(v_ref, vh_ref, cwv_ref)

    scc = scc_ref[...]
    scr = scr_ref[...]
    hpr = hpr_ref[...]
    hpc = hpc_ref[...]
    beta_c = _sigmoid(scc[:, 0:hb])
    g_c = -jnp.exp(hpr[0:1, :]) * _softplus(scc[:, hb:2 * hb] + hpr[1:2, :])
    g_r = -jnp.exp(hpc[:, 0:1]) * _softplus(scr[hb:2 * hb, :] + hpc[:, 1:2])

    ri = lax.broadcasted_iota(jnp.int32, (rows, rows), 0)
    ci = lax.broadcasted_iota(jnp.int32, (rows, rows), 1)
    tril_incl = ri >= ci
    tril_strict = ri > ci
    triu_incl = ri <= ci
    eye = (ri == ci).astype(F32)
    masks = []
    s = 1
    while s < rows:
        masks.append(((ri // (2 * s)) == (ci // (2 * s))) & (((ri // s) % 2) == 1) & (((ci // s) % 2) == 0))
        s *= 2
    gnw = gnw_ref[...]

    pr = []
    for hh in range(hb):
        cs = slice(hh * DH, (hh + 1) * DH)
        qc = q_all[:, cs]
        kc = k_all[:, cs]
        vc = v_all[:, cs]
        qc = qc * (lax.rsqrt(jnp.sum(qc * qc, axis=-1, keepdims=True) + EPS) * (DH ** -0.5))
        kc = kc * lax.rsqrt(jnp.sum(kc * kc, axis=-1, keepdims=True) + EPS)
        beta = beta_c[:, hh:hh + 1]
        gcol = jnp.sum(jnp.where(tril_incl, g_r[hh:hh + 1, :], 0.0), axis=1, keepdims=True)
        grow = jnp.sum(jnp.where(triu_incl, g_c[:, hh:hh + 1], 0.0), axis=0, keepdims=True)
        decay = jnp.exp(jnp.where(tril_incl, gcol - grow, -jnp.inf))
        kb = kc.astype(BF16)
        qkk = _dot_nt(jnp.concatenate([qc.astype(BF16), kb], axis=0), kb)
        qk = (qkk[0:rows] * decay).astype(BF16)
        a_strict = jnp.where(tril_strict, qkk[rows:2 * rows] * (decay * beta), 0.0)
        egc = jnp.exp(gcol)
        g_last = gcol[rows - 1:rows, :]
        pr.append(dict(
            a=a_strict.astype(BF16),
            t=eye - jnp.where(masks[0], a_strict, 0.0),
            rhs=jnp.concatenate([vc * beta, kc * (beta * egc)], axis=1).astype(BF16),
            qe=(qc * egc).astype(BF16),
            kd=(kc * jnp.exp(g_last - gcol)).astype(BF16),
            qk=qk, gl=jnp.exp(g_last)))

    for mk in masks[1:]:
        for p in pr:
            tb = p["t"].astype(BF16)
            y = _dot(_dot(tb, p["a"]).astype(BF16), tb)
            p["t"] = p["t"] - jnp.where(mk, y, 0.0)

    for p in pr:
        p["uw"] = _dot(p["t"].astype(BF16), p["rhs"])

    for hh, p in enumerate(pr):
        cs = slice(hh * DH, (hh + 1) * DH)
        s_h = s_ref[hh]
        uw = p["uw"]
        ws = _dot(jnp.concatenate([uw[:, DH:2 * DH].astype(BF16), p["qe"]], axis=0), s_h.astype(BF16))
        vnb = (uw[:, 0:DH] - ws[0:rows]).astype(BF16)
        o = ws[rows:2 * rows] + _dot(p["qk"], vnb)
        s_ref[hh] = s_h * p["gl"] + _dot_tn(p["kd"], vnb)
        on = o * lax.rsqrt(jnp.mean(o * o, axis=-1, keepdims=True) + EPS) * gnw
        zc = z_ref[:, cs].astype(F32)
        o_ref[:, cs] = (on * _silu(zc)).astype(o_ref.dtype)


def _gdn(p3, small, conv_w, a_log, dt_bias, gdn_norm_w):
    b, t, _ = p3.shape
    hb, rows = GDN_HB, GDN_CH
    g = HEADS // hb
    hbw = hb * DH
    nblk = HW // hbw
    bl = small[..., 0:HEADS].reshape(b, t, g, hb)
    al = small[..., HEADS:2 * HEADS].reshape(b, t, g, hb)
    scc = jnp.transpose(jnp.concatenate([bl, al], axis=-1), (0, 2, 1, 3))
    scr = jnp.transpose(scc, (0, 1, 3, 2))
    hpr = jnp.stack([a_log.reshape(g, hb), dt_bias.reshape(g, hb)], axis=1).astype(F32)
    hpc = jnp.transpose(hpr, (0, 2, 1))
    cw = conv_w.astype(F32)

    def col(off):
        return lambda bi, gi, ti: (bi, ti, off * nblk + gi)

    def halo(off):
        return lambda bi, gi, ti: (bi, jnp.maximum(ti * (rows // GDN_HALO) - 1, 0), off * nblk + gi)

    def cwmap(off):
        return lambda bi, gi, ti: (0, off * nblk + gi)

    blk = lambda off: pl.BlockSpec((None, rows, hbw), col(off))
    hblk = lambda off: pl.BlockSpec((None, GDN_HALO, hbw), halo(off))
    in_specs = [blk(0), blk(1), blk(2), blk(3), hblk(0), hblk(1), hblk(2),
                pl.BlockSpec((CONV_K, hbw), cwmap(0)),
                pl.BlockSpec((CONV_K, hbw), cwmap(1)),
                pl.BlockSpec((CONV_K, hbw), cwmap(2)),
                pl.BlockSpec((None, None, rows, 2 * hb), lambda bi, gi, ti: (bi, gi, ti, 0)),
                pl.BlockSpec((None, None, 2 * hb, rows), lambda bi, gi, ti: (bi, gi, 0, ti)),
                pl.BlockSpec((None, 2, hb), lambda bi, gi, ti: (gi, 0, 0)),
                pl.BlockSpec((None, hb, 2), lambda bi, gi, ti: (gi, 0, 0)),
                pl.BlockSpec((1, DH), lambda bi, gi, ti: (0, 0))]
    return pl.pallas_call(
        functools.partial(_gdn_kernel, hb=hb, rows=rows),
        grid=(b, g, t // rows),
        in_specs=in_specs,
        out_specs=pl.BlockSpec((None, rows, hbw), lambda bi, gi, ti: (bi, ti, gi)),
        out_shape=jax.ShapeDtypeStruct((b, t, HW), BF16),
        scratch_shapes=[pltpu.VMEM((hb, DH, DH), F32)],
        compiler_params=pltpu.CompilerParams(
            dimension_semantics=("parallel", "parallel", "arbitrary"),
            vmem_limit_bytes=_vmem_limit(8 << 20)),
        name="gdn",
    )(p3, p3, p3, p3, p3, p3, p3, cw, cw, cw, scc, scr, hpr, hpc,
      gdn_norm_w.reshape(1, DH).astype(F32))


FOX_PREP_ROWS = 512
FOX_KW = 2 * DH


def _split3(x):
    hi = x.astype(BF16)
    r1 = x - hi.astype(F32)
    mid = r1.astype(BF16)
    lo = (r1 - mid.astype(F32)).astype(BF16)
    return hi, mid, lo


def _fox_prep_kernel(q_ref, k_ref, sm_ref, bf_ref, wq_ref, wk_ref,
                     qn_ref, kn_ref, carry_ref, *, rows):
    t = pl.program_id(1)

    @pl.when(t == 0)
    def _():
        carry_ref[...] = jnp.zeros_like(carry_ref)

    x = sm_ref[:, 2 * HEADS:3 * HEADS] + bf_ref[...]
    ls = -_softplus(-x)
    ri = lax.broadcasted_iota(jnp.int32, (rows, rows), 0)
    ci = lax.broadcasted_iota(jnp.int32, (rows, rows), 1)
    tril = (ri >= ci).astype(BF16)
    hi, mid, lo = _split3(ls)
    c = _dot(tril, hi) + (_dot(tril, mid) + _dot(tril, lo)) + carry_ref[...]
    carry_ref[...] = c[rows - 1:rows, :]
    nh, nm, nl = (piece.astype(F32) for piece in _split3(-LOG2E * c))

    wq = wq_ref[...] * (DH ** -0.5 * LOG2E)
    wk = wk_ref[...]
    lane = lax.broadcasted_iota(jnp.int32, (rows, DH), 1)
    for h in range(HEADS):
        cs = slice(h * DH, (h + 1) * DH)
        q = q_ref[:, cs].astype(F32)
        k = k_ref[:, cs].astype(F32)
        qn_ref[:, cs] = (q * lax.rsqrt(jnp.mean(q * q, axis=-1, keepdims=True) + EPS) * wq).astype(qn_ref.dtype)
        kn_ref[:, h * FOX_KW:h * FOX_KW + DH] = (
            k * lax.rsqrt(jnp.mean(k * k, axis=-1, keepdims=True) + EPS) * wk).astype(kn_ref.dtype)
        extra = jnp.where(lane == 0, nh[:, h:h + 1],
                          jnp.where(lane == 1, nm[:, h:h + 1], jnp.where(lane == 2, nl[:, h:h + 1], 0.0)))
        kn_ref[:, h * FOX_KW + DH:(h + 1) * FOX_KW] = extra.astype(kn_ref.dtype)


def _fox_prep(p3, small3, fox_b_f, wq, wk):
    b, t, _ = p3.shape
    rows = _pick(t, (FOX_PREP_ROWS, 256, 128))
    return pl.pallas_call(
        functools.partial(_fox_prep_kernel, rows=rows),
        grid=(b, t // rows),
        in_specs=[pl.BlockSpec((None, rows, HW), lambda bi, ti: (bi, ti, 0)),
                  pl.BlockSpec((None, rows, HW), lambda bi, ti: (bi, ti, 1)),
                  pl.BlockSpec((None, rows, LANES), lambda bi, ti: (bi, ti, 0)),
                  pl.BlockSpec((1, HEADS), lambda bi, ti: (0, 0)),
                  pl.BlockSpec((1, DH), lambda bi, ti: (0, 0)),
                  pl.BlockSpec((1, DH), lambda bi, ti: (0, 0))],
        out_specs=[pl.BlockSpec((None, rows, HW), lambda bi, ti: (bi, ti, 0)),
                   pl.BlockSpec((None, rows, HEADS * FOX_KW), lambda bi, ti: (bi, ti, 0))],
        out_shape=[jax.ShapeDtypeStruct((b, t, HW), BF16),
                   jax.ShapeDtypeStruct((b, t, HEADS * FOX_KW), BF16)],
        scratch_shapes=[pltpu.VMEM((1, HEADS), F32)],
        compiler_params=pltpu.CompilerParams(
            dimension_semantics=("parallel", "arbitrary"),
            vmem_limit_bytes=_vmem_limit(24 << 20)),
        name="fox_prep",
    )(p3, p3, small3, fox_b_f.reshape(1, HEADS).astype(F32),
      wq.reshape(1, DH).astype(F32), wk.reshape(1, DH).astype(F32))


FOX_BQ = 2048
FOX_SUB = 256


def _fox_kernel(q_ref, k_ref, v_ref, o_ref, vp_ref, *, bq, sub):
    qi = pl.program_id(2)
    nsub = bq // sub

    @pl.when(qi == 0)
    def _():
        vp_ref[:, 0:DH] = v_ref[...]
        lane_v = lax.broadcasted_iota(jnp.int32, (vp_ref.shape[0], DH), 1)
        vp_ref[:, DH:FOX_KW] = jnp.where(lane_v == 0, 1.0, 0.0).astype(BF16)

    lane_q = lax.broadcasted_iota(jnp.int32, (sub, DH), 1)
    ones3 = jnp.where(lane_q < 3, 1.0, 0.0).astype(BF16)
    qs = [jnp.concatenate([q_ref[i * sub:(i + 1) * sub, :], ones3], axis=1) for i in range(nsub)]

    def step(carry, off, widths, masks):
        ss = [_dot_nt(qs[i], k_ref[pl.ds(off, widths[i]), :]) for i in range(nsub)]
        if masks is not None:
            ss = [jnp.where(masks[i], ss[i], -jnp.inf) for i in range(nsub)]
        ms = [jnp.maximum(carry[i][0], jnp.max(ss[i], axis=-1, keepdims=True)) for i in range(nsub)]
        ps = [jnp.exp2(ss[i] - ms[i]).astype(BF16) for i in range(nsub)]
        pvs = [_dot(ps[i], vp_ref[pl.ds(off, widths[i]), :]) for i in range(nsub)]
        return tuple((ms[i], jnp.exp2(carry[i][0] - ms[i]) * carry[i][1] + pvs[i]) for i in range(nsub))

    def body(j, carry):
        return step(carry, pl.multiple_of(j * bq, bq), [bq] * nsub, None)

    init = tuple((jnp.full((sub, 1), -jnp.inf, F32), jnp.zeros((sub, FOX_KW), F32)) for _ in range(nsub))
    carry = lax.fori_loop(0, qi, body, init)

    widths = [(i + 1) * sub for i in range(nsub)]
    masks = [(lax.broadcasted_iota(jnp.int32, (sub, w), 0) + i * sub) >= lax.broadcasted_iota(jnp.int32, (sub, w), 1)
             for i, w in enumerate(widths)]
    carry = step(carry, pl.multiple_of(qi * bq, bq), widths, masks)
    for i in range(nsub):
        acc = carry[i][1]
        o_ref[i * sub:(i + 1) * sub, :] = (acc[:, 0:DH] * (1.0 / acc[:, DH:DH + 1])).astype(o_ref.dtype)


def _fox(qn, kn, p3):
    b, t, _ = qn.shape
    bq = _pick(t, (FOX_BQ, 256, 128))
    sub = min(FOX_SUB, bq)
    nq = t // bq
    voff = 2 * HW // DH
    return pl.pallas_call(
        functools.partial(_fox_kernel, bq=bq, sub=sub),
        grid=(b, HEADS, nq),
        in_specs=[pl.BlockSpec((None, bq, DH), lambda bi, hi, qi: (bi, qi, hi)),
                  pl.BlockSpec((None, t, FOX_KW), lambda bi, hi, qi: (bi, 0, hi)),
                  pl.BlockSpec((None, t, DH), lambda bi, hi, qi: (bi, 0, voff + hi))],
        out_specs=pl.BlockSpec((None, bq, DH), lambda bi, hi, qi: (bi, qi, hi)),
        out_shape=jax.ShapeDtypeStruct((b, t, HW), BF16),
        scratch_shapes=[pltpu.VMEM((t, FOX_KW), BF16)],
        compiler_params=pltpu.CompilerParams(
            dimension_semantics=("parallel", "parallel", "arbitrary"),
            vmem_limit_bytes=_vmem_limit(40 << 20)),
        name="fox",
    )(qn, kn, p3)


def _merge_kernel(oa_ref, ob_ref, wa_ref, wb_ref, ga_ref, gb_ref, o_ref):
    ya = _dot(oa_ref[...], wa_ref[...])
    yb = _dot(ob_ref[...], wb_ref[...])
    o_ref[...] = (_sigmoid(ga_ref[...].astype(F32)) * ya
                  + _sigmoid(gb_ref[...].astype(F32)) * yb).astype(o_ref.dtype)


def _merge(oa, ob, wa, wb, gates):
    m, k = oa.shape
    n = wa.shape[1]
    bm = _pick(m, (1024, 512, 256, 128, 64, 8))
    bn = _pick(n, (1024, 512, 256, 128))
    gb0 = n // bn
    nbytes = 2 * (2 * bm * k * 2 + 2 * k * bn * 2 + 3 * bm * bn * 2)
    return pl.pallas_call(
        _merge_kernel,
        grid=(m // bm, n // bn),
        in_specs=[pl.BlockSpec((bm, k), lambda i, j: (i, 0)),
                  pl.BlockSpec((bm, k), lambda i, j: (i, 0)),
                  pl.BlockSpec((k, bn), lambda i, j: (0, j)),
                  pl.BlockSpec((k, bn), lambda i, j: (0, j)),
                  pl.BlockSpec((bm, bn), lambda i, j: (i, j)),
                  pl.BlockSpec((bm, bn), lambda i, j: (i, gb0 + j))],
        out_specs=pl.BlockSpec((bm, bn), lambda i, j: (i, j)),
        out_shape=jax.ShapeDtypeStruct((m, n), BF16),
        compiler_params=pltpu.CompilerParams(
            dimension_semantics=("parallel", "parallel"),
            vmem_limit_bytes=_vmem_limit(nbytes)),
        name="merge",
    )(oa, ob, wa, wb, gates, gates)


def _ffn1_kernel(x_ref, wg_ref, wu_ref, o_ref):
    x = x_ref[...]
    g = _dot(x, wg_ref[...])
    u = _dot(x, wu_ref[...])
    o_ref[...] = (_silu(g) * u).astype(o_ref.dtype)


def _ffn1(hn, wg, wu):
    m, k = hn.shape
    n = wg.shape[1]
    bm = _pick(m, (2048, 1024, 512, 256, 128, 64, 8))
    bn = _pick(n, (512, 256, 128))
    nbytes = 2 * (bm * k * 2 + 2 * k * bn * 2 + bm * bn * 2) + 2 * bm * bn * 4
    return pl.pallas_call(
        _ffn1_kernel,
        grid=(m // bm, n // bn),
        in_specs=[pl.BlockSpec((bm, k), lambda i, j: (i, 0)),
                  pl.BlockSpec((k, bn), lambda i, j: (0, j)),
                  pl.BlockSpec((k, bn), lambda i, j: (0, j))],
        out_specs=pl.BlockSpec((bm, bn), lambda i, j: (i, j)),
        out_shape=jax.ShapeDtypeStruct((m, n), BF16),
        compiler_params=pltpu.CompilerParams(
            dimension_semantics=("parallel", "parallel"),
            vmem_limit_bytes=_vmem_limit(nbytes)),
        name="ffn_up",
    )(hn, wg, wu)


def _ffn2_kernel(a_ref, w_ref, r_ref, o_ref, acc_ref):
    kk = pl.program_id(2)

    @pl.when(kk == 0)
    def _():
        acc_ref[...] = r_ref[...]

    acc_ref[...] += _dot(a_ref[...], w_ref[...])

    @pl.when(kk == pl.num_programs(2) - 1)
    def _():
        o_ref[...] = acc_ref[...]


def _ffn2(act, wd, h):
    m, k = act.shape
    n = wd.shape[1]
    bm = _pick(m, (1024, 512, 256, 128, 64, 8))
    bn = _pick(n, (512, 256, 128))
    bk = _pick(k, (5504, 2048, 1024, 512, 256, 128))
    nbytes = 2 * (bm * bk * 2 + bk * bn * 2 + 2 * bm * bn * 4) + bm * bn * 4
    return pl.pallas_call(
        _ffn2_kernel,
        grid=(m // bm, n // bn, k // bk),
        in_specs=[pl.BlockSpec((bm, bk), lambda i, j, kk: (i, kk)),
                  pl.BlockSpec((bk, bn), lambda i, j, kk: (kk, j)),
                  pl.BlockSpec((bm, bn), lambda i, j, kk: (i, j))],
        out_specs=pl.BlockSpec((bm, bn), lambda i, j, kk: (i, j)),
        out_shape=jax.ShapeDtypeStruct((m, n), F32),
        scratch_shapes=[pltpu.VMEM((bm, bn), F32)],
        compiler_params=pltpu.CompilerParams(
            dimension_semantics=("parallel", "parallel", "arbitrary"),
            vmem_limit_bytes=_vmem_limit(nbytes)),
        name="ffn_down",
    )(act, wd, h)


def _layer(x, norm_mix_w, w_in, conv_w, a_log, dt_bias, gdn_norm_w, fox_b_f,
           fox_q_norm_w, fox_k_norm_w, w_branch_a, w_branch_b, w_out,
           norm_ffn_w, w_ffn_gate, w_ffn_up, w_ffn_down):
    b, t, d = x.shape
    m = b * t
    x2 = x.reshape(m, d)

    o_small_a = 4 * HW
    o_b = o_small_a + 2 * HEADS
    o_f = o_b + 3 * HW
    o_gate = o_f + HEADS
    w_gdn = w_in[:, 0:o_small_a].astype(BF16)
    w_fox = w_in[:, o_b:o_f].astype(BF16)
    w_gate = w_in[:, o_gate:].astype(BF16)
    w_small = jnp.concatenate([w_in[:, o_small_a:o_b], w_in[:, o_f:o_gate]], axis=1)
    w_small = jnp.pad(w_small, ((0, 0), (0, LANES - 3 * HEADS))).astype(BF16)

    xn = _rmsnorm(x2, norm_mix_w, "norm_mix")
    p_gdn = _matmul(xn, w_gdn, BF16, "in_proj_gdn").reshape(b, t, 4 * HW)
    p_fox = _matmul(xn, w_fox, BF16, "in_proj_fox").reshape(b, t, 3 * HW)
    p_gate = _matmul(xn, w_gate, BF16, "in_proj_gate")
    small3 = _matmul(xn, w_small, F32, "in_proj_small").reshape(b, t, LANES)

    o_a = _gdn(p_gdn, small3, conv_w, a_log, dt_bias, gdn_norm_w)
    qn, kn = _fox_prep(p_fox, small3, fox_b_f, fox_q_norm_w, fox_k_norm_w)
    o_b_ = _fox(qn, kn, p_fox)

    merged = _merge(o_a.reshape(m, HW), o_b_.reshape(m, HW),
                    w_branch_a.astype(BF16), w_branch_b.astype(BF16), p_gate)
    h = _matmul(merged, w_out.astype(BF16), F32, "out_proj", residual=x2)

    hn = _rmsnorm(h, norm_ffn_w, "norm_ffn")
    act = _ffn1(hn, w_ffn_gate.astype(BF16), w_ffn_up.astype(BF16))
    out = _ffn2(act, w_ffn_down.astype(BF16), h)
    return out.reshape(b, t, d)


def kernel(x, norm_mix_w, w_in, conv_w, a_log, dt_bias, gdn_norm_w, fox_b_f, fox_q_norm_w, fox_k_norm_w, w_branch_a, w_branch_b, w_out, norm_ffn_w, w_ffn_gate, w_ffn_up, w_ffn_down):
    h = x
    for l in range(norm_mix_w.shape[0]):
        h = _layer(h, norm_mix_w[l], w_in[l], conv_w[l], a_log[l], dt_bias[l], gdn_norm_w[l],
                   fox_b_f[l], fox_q_norm_w[l], fox_k_norm_w[l], w_branch_a[l], w_branch_b[l],
                   w_out[l], norm_ffn_w[l], w_ffn_gate[l], w_ffn_up[l], w_ffn_down[l])
    return h
```

```python
import functools

import jax
import jax.numpy as jnp
from jax import lax
from jax.experimental import pallas as pl
from jax.experimental.pallas import tpu as pltpu

F32 = jnp.float32
BF16 = jnp.bfloat16

EPS = 1e-6
HEADS = 16
DH = 128
HW = HEADS * DH
CONV_K = 4

V7X_VMEM_BYTES = 64 * 1024 * 1024
LANES = 128
SUBLANES = 8
LOG2E = 1.4426950408889634


def _vmem_limit(nbytes):
    return int(min(nbytes + (16 << 20), V7X_VMEM_BYTES - (4 << 20)))


def _pick(n, prefs):
    for p in prefs:
        if n % p == 0:
            return p
    return n


def _sigmoid(x):
    return 1.0 / (1.0 + jnp.exp(-x))


def _silu(x):
    return x * _sigmoid(x)


def _softplus(x):
    return jnp.maximum(x, 0.0) + jnp.log(1.0 + jnp.exp(-jnp.abs(x)))


def _dot(a, b):
    return jnp.dot(a, b, preferred_element_type=F32)


def _dot_nt(a, b):
    return lax.dot_general(a, b, (((1,), (1,)), ((), ())), preferred_element_type=F32)


def _dot_tn(a, b):
    return lax.dot_general(a, b, (((0,), (0,)), ((), ())), preferred_element_type=F32)


def _rmsnorm_kernel(x_ref, w_ref, o_ref):
    x = x_ref[...]
    ms = jnp.mean(x * x, axis=-1, keepdims=True)
    o_ref[...] = (x * lax.rsqrt(ms + EPS) * w_ref[...]).astype(o_ref.dtype)


def _rmsnorm(x, w, name):
    m, d = x.shape
    bm = _pick(m, (256, 128, 64, 8))
    return pl.pallas_call(
        _rmsnorm_kernel,
        grid=(m // bm,),
        in_specs=[pl.BlockSpec((bm, d), lambda i: (i, 0)),
                  pl.BlockSpec((1, d), lambda i: (0, 0))],
        out_specs=pl.BlockSpec((bm, d), lambda i: (i, 0)),
        out_shape=jax.ShapeDtypeStruct((m, d), BF16),
        compiler_params=pltpu.CompilerParams(
            dimension_semantics=("parallel",),
            vmem_limit_bytes=_vmem_limit(2 * bm * d * 6)),
        name=name,
    )(x, w.reshape(1, d).astype(F32))


def _split_cast_kernel(w_ref, *o_refs, starts):
    for o_ref, start in zip(o_refs, starts):
        o_ref[...] = w_ref[:, start:start + o_ref.shape[1]].astype(o_ref.dtype)


def _split_cast(w, regions, name):
    k, n = w.shape
    bk = _pick(k, (64, 32, 16))
    return pl.pallas_call(
        functools.partial(_split_cast_kernel, starts=tuple(s for s, _ in regions)),
        grid=(k // bk,),
        in_specs=[pl.BlockSpec((bk, n), lambda i: (i, 0))],
        out_specs=[pl.BlockSpec((bk, wd), lambda i: (i, 0)) for _, wd in regions],
        out_shape=[jax.ShapeDtypeStruct((k, wd), BF16) for _, wd in regions],
        compiler_params=pltpu.CompilerParams(
            dimension_semantics=("parallel",),
            vmem_limit_bytes=_vmem_limit(2 * bk * n * 6)),
        name=name,
    )(w)


def _mm_kernel(x_ref, w_ref, o_ref):
    o_ref[...] = _dot(x_ref[...], w_ref[...]).astype(o_ref.dtype)


def _mm_res_kernel(x_ref, w_ref, r_ref, o_ref):
    o_ref[...] = (r_ref[...] + _dot(x_ref[...], w_ref[...])).astype(o_ref.dtype)


def _matmul(x, w, out_dtype, name, residual=None, bm_prefs=(1024, 512, 256, 128, 64, 8),
            bn_prefs=(1024, 512, 256, 128)):
    m, k = x.shape
    _, n = w.shape
    bm = _pick(m, bm_prefs)
    bn = _pick(n, bn_prefs)
    osz = jnp.dtype(out_dtype).itemsize
    nbytes = 2 * (bm * k * 2 + k * bn * 2 + bm * bn * osz)
    in_specs = [pl.BlockSpec((bm, k), lambda i, j: (i, 0)),
                pl.BlockSpec((k, bn), lambda i, j: (0, j))]
    args = [x, w]
    kern = _mm_kernel
    if residual is not None:
        in_specs.append(pl.BlockSpec((bm, bn), lambda i, j: (i, j)))
        args.append(residual)
        kern = _mm_res_kernel
        nbytes += 2 * bm * bn * 4
    return pl.pallas_call(
        kern,
        grid=(m // bm, n // bn),
        in_specs=in_specs,
        out_specs=pl.BlockSpec((bm, bn), lambda i, j: (i, j)),
        out_shape=jax.ShapeDtypeStruct((m, n), out_dtype),
        compiler_params=pltpu.CompilerParams(
            dimension_semantics=("parallel", "parallel"),
            vmem_limit_bytes=_vmem_limit(nbytes)),
        name=name,
    )(*args)


GDN_CH = 128
GDN_HB = 16
GDN_HALO = 16


def _gdn_kernel(q_ref, k_ref, v_ref, z_ref, qh_ref, kh_ref, vh_ref,
                cwq_ref, cwk_ref, cwv_ref, scc_ref, scr_ref, hpr_ref, hpc_ref, gnw_ref,
                o_ref, s_ref, *, hb, rows):
    t = pl.program_id(2)

    @pl.when(t == 0)
    def _():
        s_ref[...] = jnp.zeros_like(s_ref)

    sr = lax.broadcasted_iota(jnp.int32, (3 * rows, rows + GDN_HALO), 0)
    sc = lax.broadcasted_iota(jnp.int32, (3 * rows, rows + GDN_HALO), 1)
    shift = (sc == (sr % rows) + (sr // rows) + (GDN_HALO - 3)).astype(BF16)

    def conv_silu(x_ref, h_ref, cw_ref):
        x = x_ref[...]
        halo = jnp.where(t > 0, h_ref[...], jnp.zeros_like(h_ref))
        sh = _dot(shift, jnp.concatenate([halo, x], axis=0))
        cw = cw_ref[...]
        y = x.astype(F32) * cw[CONV_K - 1:CONV_K, :]
        for i in range(CONV_K - 1):
            y = y + sh[i * rows:(i + 1) * rows] * cw[i:i + 1, :]
        return _silu(y)

    q_all = conv_silu(q_ref, qh_ref, cwq_ref)
    k_all = conv_silu(k_ref, kh_ref, cwk_ref)
    v_all = conv_silu(v_ref, vh_ref, cwv_ref)

    scc = scc_ref[...]
    scr = scr_ref[...]
    hpr = hpr_ref[...]
    hpc = hpc_ref[...]
    beta_c = _sigmoid(scc[:, 0:hb])
    g_c = -jnp.exp(hpr[0:1, :]) * _softplus(scc[:, hb:2 * hb] + hpr[1:2, :])
    g_r = -jnp.exp(hpc[:, 0:1]) * _softplus(scr[hb:2 * hb, :] + hpc[:, 1:2])

    ri = lax.broadcasted_iota(jnp.int32, (rows, rows), 0)
    ci = lax.broadcasted_iota(jnp.int32, (rows, rows), 1)
    tril_incl = ri >= ci
    tril_strict = ri > ci
    triu_incl = ri <= ci
    eye = (ri == ci).astype(F32)
    masks = []
    s = 1
    while s < rows:
        masks.append(((ri // (2 * s)) == (ci // (2 * s))) & (((ri // s) % 2) == 1) & (((ci // s) % 2) == 0))
        s *= 2
    gnw = gnw_ref[...]

    pr = []
    for hh in range(hb):
        cs = slice(hh * DH, (hh + 1) * DH)
        qc = q_all[:, cs]
        kc = k_all[:, cs]
        vc = v_all[:, cs]
        qc = qc * (lax.rsqrt(jnp.sum(qc * qc, axis=-1, keepdims=True) + EPS) * (DH ** -0.5))
        kc = kc * lax.rsqrt(jnp.sum(kc * kc, axis=-1, keepdims=True) + EPS)
        beta = beta_c[:, hh:hh + 1]
        gcol = jnp.sum(jnp.where(tril_incl, g_r[hh:hh + 1, :], 0.0), axis=1, keepdims=True)
        grow = jnp.sum(jnp.where(triu_incl, g_c[:, hh:hh + 1], 0.0), axis=0, keepdims=True)
        decay = jnp.exp(jnp.where(tril_incl, gcol - grow, -jnp.inf))
        kb = kc.astype(BF16)
        qkk = _dot_nt(jnp.concatenate([qc.astype(BF16), kb], axis=0), kb)
        qk = (qkk[0:rows] * decay).astype(BF16)
        a_strict = jnp.where(tril_strict, qkk[rows:2 * rows] * (decay * beta), 0.0)
        egc = jnp.exp(gcol)
        g_last = gcol[rows - 1:rows, :]
        pr.append(dict(
            a=a_strict.astype(BF16),
            t=eye - jnp.where(masks[0], a_strict, 0.0),
            rhs=jnp.concatenate([vc * beta, kc * (beta * egc)], axis=1).astype(BF16),
            qe=(qc * egc).astype(BF16),
            kd=(kc * jnp.exp(g_last - gcol)).astype(BF16),
            qk=qk, gl=jnp.exp(g_last)))

    for mk in masks[1:]:
        tbs = [p["t"].astype(BF16) for p in pr]
        xs = [_dot(tb, p["a"]).astype(BF16) for tb, p in zip(tbs, pr)]
        ys = [_dot(x, tb) for x, tb in zip(xs, tbs)]
        for p, y in zip(pr, ys):
            p["t"] = p["t"] - jnp.where(mk, y, 0.0)

    uws = [_dot(p["t"].astype(BF16), p["rhs"]) for p in pr]

    s_old = [s_ref[hh] for hh in range(hb)]
    wss = [_dot(jnp.concatenate([uw[:, DH:2 * DH].astype(BF16), p["qe"]], axis=0), s_h.astype(BF16))
           for uw, p, s_h in zip(uws, pr, s_old)]
    vnbs = [(uw[:, 0:DH] - ws[0:rows]).astype(BF16) for uw, ws in zip(uws, wss)]
    os_ = [ws[rows:2 * rows] + _dot(p["qk"], vnb) for ws, p, vnb in zip(wss, pr, vnbs)]
    for hh, (p, vnb, s_h) in enumerate(zip(pr, vnbs, s_old)):
        s_ref[hh] = s_h * p["gl"] + _dot_tn(p["kd"], vnb)
    for hh, o in enumerate(os_):
        cs = slice(hh * DH, (hh + 1) * DH)
        on = o * lax.rsqrt(jnp.mean(o * o, axis=-1, keepdims=True) + EPS) * gnw
        zc = z_ref[:, cs].astype(F32)
        o_ref[:, cs] = (on * _silu(zc)).astype(o_ref.dtype)


def _gdn(p3, small, conv_w, a_log, dt_bias, gdn_norm_w):
    b, t, _ = p3.shape
    hb, rows = GDN_HB, GDN_CH
    g = HEADS // hb
    hbw = hb * DH
    nblk = HW // hbw
    bl = small[..., 0:HEADS].reshape(b, t, g, hb)
    al = small[..., HEADS:2 * HEADS].reshape(b, t, g, hb)
    scc = jnp.transpose(jnp.concatenate([bl, al], axis=-1), (0, 2, 1, 3))
    scr = jnp.transpose(scc, (0, 1, 3, 2))
    hpr = jnp.stack([a_log.reshape(g, hb), dt_bias.reshape(g, hb)], axis=1).astype(F32)
    hpc = jnp.transpose(hpr, (0, 2, 1))
    cw = conv_w.astype(F32)

    def col(off):
        return lambda bi, gi, ti: (bi, ti, off * nblk + gi)

    def halo(off):
        return lambda bi, gi, ti: (bi, jnp.maximum(ti * (rows // GDN_HALO) - 1, 0), off * nblk + gi)

    def cwmap(off):
        return lambda bi, gi, ti: (0, off * nblk + gi)

    blk = lambda off: pl.BlockSpec((None, rows, hbw), col(off))
    hblk = lambda off: pl.BlockSpec((None, GDN_HALO, hbw), halo(off))
    in_specs = [blk(0), blk(1), blk(2), blk(3), hblk(0), hblk(1), hblk(2),
                pl.BlockSpec((CONV_K, hbw), cwmap(0)),
                pl.BlockSpec((CONV_K, hbw), cwmap(1)),
                pl.BlockSpec((CONV_K, hbw), cwmap(2)),
                pl.BlockSpec((None, None, rows, 2 * hb), lambda bi, gi, ti: (bi, gi, ti, 0)),
                pl.BlockSpec((None, None, 2 * hb, rows), lambda bi, gi, ti: (bi, gi, 0, ti)),
                pl.BlockSpec((None, 2, hb), lambda bi, gi, ti: (gi, 0, 0)),
                pl.BlockSpec((None, hb, 2), lambda bi, gi, ti: (gi, 0, 0)),
                pl.BlockSpec((1, DH), lambda bi, gi, ti: (0, 0))]
    return pl.pallas_call(
        functools.partial(_gdn_kernel, hb=hb, rows=rows),
        grid=(b, g, t // rows),
        in_specs=in_specs,
        out_specs=pl.BlockSpec((None, rows, hbw), lambda bi, gi, ti: (bi, ti, gi)),
        out_shape=jax.ShapeDtypeStruct((b, t, HW), BF16),
        scratch_shapes=[pltpu.VMEM((hb, DH, DH), F32)],
        compiler_params=pltpu.CompilerParams(
            dimension_semantics=("parallel", "parallel", "arbitrary"),
            vmem_limit_bytes=_vmem_limit(8 << 20)),
        name="gdn",
    )(p3, p3, p3, p3, p3, p3, p3, cw, cw, cw, scc, scr, hpr, hpc,
      gdn_norm_w.reshape(1, DH).astype(F32))


FOX_PREP_ROWS = 512
FOX_KW = 2 * DH


def _split3(x):
    hi = x.astype(BF16)
    r1 = x - hi.astype(F32)
    mid = r1.astype(BF16)
    lo = (r1 - mid.astype(F32)).astype(BF16)
    return hi, mid, lo


def _fox_prep_kernel(q_ref, k_ref, sm_ref, bf_ref, wq_ref, wk_ref,
                     qn_ref, kn_ref, carry_ref, *, rows):
    t = pl.program_id(1)

    @pl.when(t == 0)
    def _():
        carry_ref[...] = jnp.zeros_like(carry_ref)

    x = sm_ref[:, 2 * HEADS:3 * HEADS] + bf_ref[...]
    ls = -_softplus(-x)
    ri = lax.broadcasted_iota(jnp.int32, (rows, rows), 0)
    ci = lax.broadcasted_iota(jnp.int32, (rows, rows), 1)
    tril = (ri >= ci).astype(BF16)
    hi, mid, lo = _split3(ls)
    c = _dot(tril, hi) + (_dot(tril, mid) + _dot(tril, lo)) + carry_ref[...]
    carry_ref[...] = c[rows - 1:rows, :]
    nh, nm, nl = (piece.astype(F32) for piece in _split3(-LOG2E * c))

    wq = wq_ref[...] * (DH ** -0.5 * LOG2E)
    wk = wk_ref[...]
    lane = lax.broadcasted_iota(jnp.int32, (rows, DH), 1)
    for h in range(HEADS):
        cs = slice(h * DH, (h + 1) * DH)
        q = q_ref[:, cs].astype(F32)
        k = k_ref[:, cs].astype(F32)
        qn_ref[:, cs] = (q * lax.rsqrt(jnp.mean(q * q, axis=-1, keepdims=True) + EPS) * wq).astype(qn_ref.dtype)
        kn_ref[:, h * FOX_KW:h * FOX_KW + DH] = (
            k * lax.rsqrt(jnp.mean(k * k, axis=-1, keepdims=True) + EPS) * wk).astype(kn_ref.dtype)
        extra = jnp.where(lane == 0, nh[:, h:h + 1],
                          jnp.where(lane == 1, nm[:, h:h + 1], jnp.where(lane == 2, nl[:, h:h + 1], 0.0)))
        kn_ref[:, h * FOX_KW + DH:(h + 1) * FOX_KW] = extra.astype(kn_ref.dtype)


def _fox_prep(p3, small3, fox_b_f, wq, wk):
    b, t, _ = p3.shape
    rows = _pick(t, (FOX_PREP_ROWS, 256, 128))
    return pl.pallas_call(
        functools.partial(_fox_prep_kernel, rows=rows),
        grid=(b, t // rows),
        in_specs=[pl.BlockSpec((None, rows, HW), lambda bi, ti: (bi, ti, 0)),
                  pl.BlockSpec((None, rows, HW), lambda bi, ti: (bi, ti, 1)),
                  pl.BlockSpec((None, rows, LANES), lambda bi, ti: (bi, ti, 0)),
                  pl.BlockSpec((1, HEADS), lambda bi, ti: (0, 0)),
                  pl.BlockSpec((1, DH), lambda bi, ti: (0, 0)),
                  pl.BlockSpec((1, DH), lambda bi, ti: (0, 0))],
        out_specs=[pl.BlockSpec((None, rows, HW), lambda bi, ti: (bi, ti, 0)),
                   pl.BlockSpec((None, rows, HEADS * FOX_KW), lambda bi, ti: (bi, ti, 0))],
        out_shape=[jax.ShapeDtypeStruct((b, t, HW), BF16),
                   jax.ShapeDtypeStruct((b, t, HEADS * FOX_KW), BF16)],
        scratch_shapes=[pltpu.VMEM((1, HEADS), F32)],
        compiler_params=pltpu.CompilerParams(
            dimension_semantics=("parallel", "arbitrary"),
            vmem_limit_bytes=_vmem_limit(24 << 20)),
        name="fox_prep",
    )(p3, p3, small3, fox_b_f.reshape(1, HEADS).astype(F32),
      wq.reshape(1, DH).astype(F32), wk.reshape(1, DH).astype(F32))


FOX_BQ = 2048
FOX_SUB = 256


def _fox_kernel(q_ref, k_ref, v_ref, o_ref, vp_ref, *, bq, sub):
    qi = pl.program_id(2)
    nsub = bq // sub

    @pl.when(qi == 0)
    def _():
        vp_ref[:, 0:DH] = v_ref[...]
        lane_v = lax.broadcasted_iota(jnp.int32, (vp_ref.shape[0], DH), 1)
        vp_ref[:, DH:FOX_KW] = jnp.where(lane_v == 0, 1.0, 0.0).astype(BF16)

    lane_q = lax.broadcasted_iota(jnp.int32, (sub, DH), 1)
    ones3 = jnp.where(lane_q < 3, 1.0, 0.0).astype(BF16)
    qs = [jnp.concatenate([q_ref[i * sub:(i + 1) * sub, :], ones3], axis=1) for i in range(nsub)]

    def step(carry, off, widths, masks):
        ss = [_dot_nt(qs[i], k_ref[pl.ds(off, widths[i]), :]) for i in range(nsub)]
        if masks is not None:
            ss = [jnp.where(masks[i], ss[i], -jnp.inf) for i in range(nsub)]
        ms = [jnp.maximum(carry[i][0], jnp.max(ss[i], axis=-1, keepdims=True)) for i in range(nsub)]
        ps = [jnp.exp2(ss[i] - ms[i]).astype(BF16) for i in range(nsub)]
        pvs = [_dot(ps[i], vp_ref[pl.ds(off, widths[i]), :]) for i in range(nsub)]
        return tuple((ms[i], jnp.exp2(carry[i][0] - ms[i]) * carry[i][1] + pvs[i]) for i in range(nsub))

    def body(j, carry):
        return step(carry, pl.multiple_of(j * bq, bq), [bq] * nsub, None)

    init = tuple((jnp.full((sub, 1), -jnp.inf, F32), jnp.zeros((sub, FOX_KW), F32)) for _ in range(nsub))
    carry = lax.fori_loop(0, qi, body, init)

    widths = [(i + 1) * sub for i in range(nsub)]
    masks = [(lax.broadcasted_iota(jnp.int32, (sub, w), 0) + i * sub) >= lax.broadcasted_iota(jnp.int32, (sub, w), 1)
             for i, w in enumerate(widths)]
    carry = step(carry, pl.multiple_of(qi * bq, bq), widths, masks)
    for i in range(nsub):
        acc = carry[i][1]
        o_ref[i * sub:(i + 1) * sub, :] = (acc[:, 0:DH] * (1.0 / acc[:, DH:DH + 1])).astype(o_ref.dtype)


def _fox(qn, kn, p3):
    b, t, _ = qn.shape
    bq = _pick(t, (FOX_BQ, 256, 128))
    sub = min(FOX_SUB, bq)
    nq = t // bq
    voff = 2 * HW // DH
    return pl.pallas_call(
        functools.partial(_fox_kernel, bq=bq, sub=sub),
        grid=(b, HEADS, nq),
        in_specs=[pl.BlockSpec((None, bq, DH), lambda bi, hi, qi: (bi, qi, hi)),
                  pl.BlockSpec((None, t, FOX_KW), lambda bi, hi, qi: (bi, 0, hi)),
                  pl.BlockSpec((None, t, DH), lambda bi, hi, qi: (bi, 0, voff + hi))],
        out_specs=pl.BlockSpec((None, bq, DH), lambda bi, hi, qi: (bi, qi, hi)),
        out_shape=jax.ShapeDtypeStruct((b, t, HW), BF16),
        scratch_shapes=[pltpu.VMEM((t, FOX_KW), BF16)],
        compiler_params=pltpu.CompilerParams(
            dimension_semantics=("parallel", "parallel", "arbitrary"),
            vmem_limit_bytes=_vmem_limit(40 << 20)),
        name="fox",
    )(qn, kn, p3)


def _merge_kernel(oa_ref, ob_ref, wa_ref, wb_ref, ga_ref, gb_ref, o_ref):
    ya = _dot(oa_ref[...], wa_ref[...])
    yb = _dot(ob_ref[...], wb_ref[...])
    o_ref[...] = (_sigmoid(ga_ref[...].astype(F32)) * ya
                  + _sigmoid(gb_ref[...].astype(F32)) * yb).astype(o_ref.dtype)


def _merge(oa, ob, wa, wb, gates):
    m, k = oa.shape
    n = wa.shape[1]
    bm = _pick(m, (1024, 512, 256, 128, 64, 8))
    bn = _pick(n, (1024, 512, 256, 128))
    gb0 = n // bn
    nbytes = 2 * (2 * bm * k * 2 + 2 * k * bn * 2 + 3 * bm * bn * 2)
    return pl.pallas_call(
        _merge_kernel,
        grid=(m // bm, n // bn),
        in_specs=[pl.BlockSpec((bm, k), lambda i, j: (i, 0)),
                  pl.BlockSpec((bm, k), lambda i, j: (i, 0)),
                  pl.BlockSpec((k, bn), lambda i, j: (0, j)),
                  pl.BlockSpec((k, bn), lambda i, j: (0, j)),
                  pl.BlockSpec((bm, bn), lambda i, j: (i, j)),
                  pl.BlockSpec((bm, bn), lambda i, j: (i, gb0 + j))],
        out_specs=pl.BlockSpec((bm, bn), lambda i, j: (i, j)),
        out_shape=jax.ShapeDtypeStruct((m, n), BF16),
        compiler_params=pltpu.CompilerParams(
            dimension_semantics=("parallel", "parallel"),
            vmem_limit_bytes=_vmem_limit(nbytes)),
        name="merge",
    )(oa, ob, wa, wb, gates, gates)


def _ffn1_kernel(x_ref, wg_ref, wu_ref, o_ref):
    x = x_ref[...]
    g = _dot(x, wg_ref[...])
    u = _dot(x, wu_ref[...])
    o_ref[...] = (_silu(g) * u).astype(o_ref.dtype)


def _ffn1(hn, wg, wu):
    m, k = hn.shape
    n = wg.shape[1]
    bm = _pick(m, (2048, 1024, 512, 256, 128, 64, 8))
    bn = _pick(n, (512, 256, 128))
    nbytes = 2 * (bm * k * 2 + 2 * k * bn * 2 + bm * bn * 2) + 2 * bm * bn * 4
    return pl.pallas_call(
        _ffn1_kernel,
        grid=(m // bm, n // bn),
        in_specs=[pl.BlockSpec((bm, k), lambda i, j: (i, 0)),
                  pl.BlockSpec((k, bn), lambda i, j: (0, j)),
                  pl.BlockSpec((k, bn), lambda i, j: (0, j))],
        out_specs=pl.BlockSpec((bm, bn), lambda i, j: (i, j)),
        out_shape=jax.ShapeDtypeStruct((m, n), BF16),
        compiler_params=pltpu.CompilerParams(
            dimension_semantics=("parallel", "parallel"),
            vmem_limit_bytes=_vmem_limit(nbytes)),
        name="ffn_up",
    )(hn, wg, wu)


def _layer(x, norm_mix_w, w_in, conv_w, a_log, dt_bias, gdn_norm_w, fox_b_f,
           fox_q_norm_w, fox_k_norm_w, w_branch_a, w_branch_b, w_out,
           norm_ffn_w, w_ffn_gate, w_ffn_up, w_ffn_down):
    b, t, d = x.shape
    m = b * t
    x2 = x.reshape(m, d)

    o_small_a = 4 * HW
    o_b = o_small_a + 2 * HEADS
    o_f = o_b + 3 * HW
    o_gate = o_f + HEADS
    w_gdn, w_fox, w_gate = _split_cast(
        w_in, [(0, o_small_a), (o_b, o_f - o_b), (o_gate, w_in.shape[1] - o_gate)], "w_in_cast")
    w_small = jnp.concatenate([w_in[:, o_small_a:o_b], w_in[:, o_f:o_gate]], axis=1)
    w_small = jnp.pad(w_small, ((0, 0), (0, LANES - 3 * HEADS))).astype(BF16)

    xn = _rmsnorm(x2, norm_mix_w, "norm_mix")
    p_gdn = _matmul(xn, w_gdn, BF16, "in_proj_gdn").reshape(b, t, 4 * HW)
    p_fox = _matmul(xn, w_fox, BF16, "in_proj_fox").reshape(b, t, 3 * HW)
    p_gate = _matmul(xn, w_gate, BF16, "in_proj_gate")
    small3 = _matmul(xn, w_small, F32, "in_proj_small").reshape(b, t, LANES)

    o_a = _gdn(p_gdn, small3, conv_w, a_log, dt_bias, gdn_norm_w)
    qn, kn = _fox_prep(p_fox, small3, fox_b_f, fox_q_norm_w, fox_k_norm_w)
    o_b_ = _fox(qn, kn, p_fox)

    merged = _merge(o_a.reshape(m, HW), o_b_.reshape(m, HW),
                    w_branch_a.astype(BF16), w_branch_b.astype(BF16), p_gate)
    h = _matmul(merged, w_out.astype(BF16), F32, "out_proj", residual=x2)

    hn = _rmsnorm(h, norm_ffn_w, "norm_ffn")
    act = _ffn1(hn, w_ffn_gate.astype(BF16), w_ffn_up.astype(BF16))
    out = _matmul(act, w_ffn_down.astype(BF16), F32, "ffn_down", residual=h,
                  bm_prefs=(512, 256, 128, 64, 8), bn_prefs=(512, 256, 128))
    return out.reshape(b, t, d)


def kernel(x, norm_mix_w, w_in, conv_w, a_log, dt_bias, gdn_norm_w, fox_b_f, fox_q_norm_w, fox_k_norm_w, w_branch_a, w_branch_b, w_out, norm_ffn_w, w_ffn_gate, w_ffn_up, w_ffn_down):
    h = x
    for l in range(norm_mix_w.shape[0]):
        h = _layer(h, norm_mix_w[l], w_in[l], conv_w[l], a_log[l], dt_bias[l], gdn_norm_w[l],
                   fox_b_f[l], fox_q_norm_w[l], fox_k_norm_w[l], w_branch_a[l], w_branch_b[l],
                   w_out[l], norm_ffn_w[l], w_ffn_gate[l], w_ffn_up[l], w_ffn_down[l])
    return h
```

```python
import functools

import jax
import jax.numpy as jnp
from jax import lax
from jax.experimental import pallas as pl
from jax.experimental.pallas import tpu as pltpu

F32 = jnp.float32
BF16 = jnp.bfloat16

EPS = 1e-6
HEADS = 16
DH = 128
HW = HEADS * DH
CONV_K = 4

V7X_VMEM_BYTES = 64 * 1024 * 1024
LANES = 128
SUBLANES = 8
BF16_SUBLANES = 16
LOG2E = 1.4426950408889634


def _vmem_limit(nbytes):
    return int(min(nbytes + (16 << 20), V7X_VMEM_BYTES - (4 << 20)))


def _pick(n, prefs):
    for p in prefs:
        if n % p == 0:
            return p
    return n


def _sigmoid(x):
    return 1.0 / (1.0 + jnp.exp(-x))


def _silu(x):
    return x * _sigmoid(x)


def _softplus(x):
    return jnp.maximum(x, 0.0) + jnp.log(1.0 + jnp.exp(-jnp.abs(x)))


def _dot(a, b):
    return jnp.dot(a, b, preferred_element_type=F32)


def _dot_nt(a, b):
    return lax.dot_general(a, b, (((1,), (1,)), ((), ())), preferred_element_type=F32)


def _dot_tn(a, b):
    return lax.dot_general(a, b, (((0,), (0,)), ((), ())), preferred_element_type=F32)


def _rmsnorm_kernel(x_ref, w_ref, o_ref):
    x = x_ref[...]
    ms = jnp.mean(x * x, axis=-1, keepdims=True)
    o_ref[...] = (x * lax.rsqrt(ms + EPS) * w_ref[...]).astype(o_ref.dtype)


def _rmsnorm(x, w, name):
    m, d = x.shape
    bm = _pick(m, (256, 128, 64, 8))
    return pl.pallas_call(
        _rmsnorm_kernel,
        grid=(m // bm,),
        in_specs=[pl.BlockSpec((bm, d), lambda i: (i, 0)),
                  pl.BlockSpec((1, d), lambda i: (0, 0))],
        out_specs=pl.BlockSpec((bm, d), lambda i: (i, 0)),
        out_shape=jax.ShapeDtypeStruct((m, d), BF16),
        compiler_params=pltpu.CompilerParams(
            dimension_semantics=("parallel",),
            vmem_limit_bytes=_vmem_limit(2 * bm * d * 6)),
        name=name,
    )(x, w.reshape(1, d).astype(F32))


def _mm_nt_kernel(x_ref, wt_ref, o_ref):
    o_ref[...] = _dot_nt(x_ref[...], wt_ref[...]).astype(o_ref.dtype)


def _matmul_nt(x, wt, row0, n, out_dtype, name):
    m, k = x.shape
    bm = _pick(m, (1024, 512, 256, 128, 64, 8))
    bn = _pick(n, (1024, 512, 256, 128))
    nbytes = 2 * (bm * k * 2 + k * bn * 2 + bm * bn * jnp.dtype(out_dtype).itemsize)
    return pl.pallas_call(
        _mm_nt_kernel,
        grid=(m // bm, n // bn),
        in_specs=[pl.BlockSpec((bm, k), lambda i, j: (i, 0)),
                  pl.BlockSpec((pl.Element(bn), pl.Element(k)),
                               lambda i, j: (pl.multiple_of(row0 + j * bn, BF16_SUBLANES), 0))],
        out_specs=pl.BlockSpec((bm, bn), lambda i, j: (i, j)),
        out_shape=jax.ShapeDtypeStruct((m, n), out_dtype),
        compiler_params=pltpu.CompilerParams(
            dimension_semantics=("parallel", "parallel"),
            vmem_limit_bytes=_vmem_limit(nbytes)),
        name=name,
    )(x, wt)


def _mm_kernel(x_ref, w_ref, o_ref):
    o_ref[...] = _dot(x_ref[...], w_ref[...]).astype(o_ref.dtype)


def _mm_res_kernel(x_ref, w_ref, r_ref, o_ref):
    o_ref[...] = (r_ref[...] + _dot(x_ref[...], w_ref[...])).astype(o_ref.dtype)


def _matmul(x, w, out_dtype, name, residual=None, bm_prefs=(1024, 512, 256, 128, 64, 8),
            bn_prefs=(1024, 512, 256, 128)):
    m, k = x.shape
    _, n = w.shape
    bm = _pick(m, bm_prefs)
    bn = _pick(n, bn_prefs)
    osz = jnp.dtype(out_dtype).itemsize
    nbytes = 2 * (bm * k * 2 + k * bn * 2 + bm * bn * osz)
    in_specs = [pl.BlockSpec((bm, k), lambda i, j: (i, 0)),
                pl.BlockSpec((k, bn), lambda i, j: (0, j))]
    args = [x, w]
    kern = _mm_kernel
    if residual is not None:
        in_specs.append(pl.BlockSpec((bm, bn), lambda i, j: (i, j)))
        args.append(residual)
        kern = _mm_res_kernel
        nbytes += 2 * bm * bn * 4
    return pl.pallas_call(
        kern,
        grid=(m // bm, n // bn),
        in_specs=in_specs,
        out_specs=pl.BlockSpec((bm, bn), lambda i, j: (i, j)),
        out_shape=jax.ShapeDtypeStruct((m, n), out_dtype),
        compiler_params=pltpu.CompilerParams(
            dimension_semantics=("parallel", "parallel"),
            vmem_limit_bytes=_vmem_limit(nbytes)),
        name=name,
    )(*args)


GDN_CH = 128
GDN_HB = 16
GDN_HALO = 16


def _gdn_kernel(q_ref, k_ref, v_ref, z_ref, qh_ref, kh_ref, vh_ref,
                cwq_ref, cwk_ref, cwv_ref, scc_ref, scr_ref, hpr_ref, hpc_ref, gnw_ref,
                o_ref, s_ref, *, hb, rows):
    t = pl.program_id(2)

    @pl.when(t == 0)
    def _():
        s_ref[...] = jnp.zeros_like(s_ref)

    sr = lax.broadcasted_iota(jnp.int32, (3 * rows, rows + GDN_HALO), 0)
    sc = lax.broadcasted_iota(jnp.int32, (3 * rows, rows + GDN_HALO), 1)
    shift = (sc == (sr % rows) + (sr // rows) + (GDN_HALO - 3)).astype(BF16)

    def conv_silu(x_ref, h_ref, cw_ref):
        x = x_ref[...]
        halo = jnp.where(t > 0, h_ref[...], jnp.zeros_like(h_ref))
        sh = _dot(shift, jnp.concatenate([halo, x], axis=0))
        cw = cw_ref[...]
        y = x.astype(F32) * cw[CONV_K - 1:CONV_K, :]
        for i in range(CONV_K - 1):
            y = y + sh[i * rows:(i + 1) * rows] * cw[i:i + 1, :]
        return _silu(y)

    q_all = conv_silu(q_ref, qh_ref, cwq_ref)
    k_all = conv_silu(k_ref, kh_ref, cwk_ref)
    v_all = conv_silu(v_ref, vh_ref, cwv_ref)

    scc = scc_ref[...]
    scr = scr_ref[...]
    hpr = hpr_ref[...]
    hpc = hpc_ref[...]
    beta_c = _sigmoid(scc[:, 0:hb])
    g_c = -jnp.exp(hpr[0:1, :]) * _softplus(scc[:, hb:2 * hb] + hpr[1:2, :])
    g_r = -jnp.exp(hpc[:, 0:1]) * _softplus(scr[hb:2 * hb, :] + hpc[:, 1:2])

    ri = lax.broadcasted_iota(jnp.int32, (rows, rows), 0)
    ci = lax.broadcasted_iota(jnp.int32, (rows, rows), 1)
    tril_incl = ri >= ci
    tril_strict = ri > ci
    triu_incl = ri <= ci
    eye = (ri == ci).astype(F32)
    masks = []
    s = 1
    while s < rows:
        masks.append(((ri // (2 * s)) == (ci // (2 * s))) & (((ri // s) % 2) == 1) & (((ci // s) % 2) == 0))
        s *= 2
    gnw = gnw_ref[...]

    pr = []
    for hh in range(hb):
        cs = slice(hh * DH, (hh + 1) * DH)
        qc = q_all[:, cs]
        kc = k_all[:, cs]
        vc = v_all[:, cs]
        qc = qc * (lax.rsqrt(jnp.sum(qc * qc, axis=-1, keepdims=True) + EPS) * (DH ** -0.5))
        kc = kc * lax.rsqrt(jnp.sum(kc * kc, axis=-1, keepdims=True) + EPS)
        beta = beta_c[:, hh:hh + 1]
        gcol = jnp.sum(jnp.where(tril_incl, g_r[hh:hh + 1, :], 0.0), axis=1, keepdims=True)
        grow = jnp.sum(jnp.where(triu_incl, g_c[:, hh:hh + 1], 0.0), axis=0, keepdims=True)
        decay = jnp.exp(jnp.where(tril_incl, gcol - grow, -jnp.inf))
        kb = kc.astype(BF16)
        qkk = _dot_nt(jnp.concatenate([qc.astype(BF16), kb], axis=0), kb)
        qk = (qkk[0:rows] * decay).astype(BF16)
        a_strict = jnp.where(tril_strict, qkk[rows:2 * rows] * (decay * beta), 0.0)
        egc = jnp.exp(gcol)
        g_last = gcol[rows - 1:rows, :]
        pr.append(dict(
            a=a_strict.astype(BF16),
            t=eye - jnp.where(masks[0], a_strict, 0.0),
            rhs=jnp.concatenate([vc * beta, kc * (beta * egc)], axis=1).astype(BF16),
            qe=(qc * egc).astype(BF16),
            kd=(kc * jnp.exp(g_last - gcol)).astype(BF16),
            qk=qk, gl=jnp.exp(g_last)))

    for mk in masks[1:]:
        tbs = [p["t"].astype(BF16) for p in pr]
        xs = [_dot(tb, p["a"]).astype(BF16) for tb, p in zip(tbs, pr)]
        ys = [_dot(x, tb) for x, tb in zip(xs, tbs)]
        for p, y in zip(pr, ys):
            p["t"] = p["t"] - jnp.where(mk, y, 0.0)

    uws = [_dot(p["t"].astype(BF16), p["rhs"]) for p in pr]

    s_old = [s_ref[hh] for hh in range(hb)]
    wss = [_dot(jnp.concatenate([uw[:, DH:2 * DH].astype(BF16), p["qe"]], axis=0), s_h.astype(BF16))
           for uw, p, s_h in zip(uws, pr, s_old)]
    vnbs = [(uw[:, 0:DH] - ws[0:rows]).astype(BF16) for uw, ws in zip(uws, wss)]
    os_ = [ws[rows:2 * rows] + _dot(p["qk"], vnb) for ws, p, vnb in zip(wss, pr, vnbs)]
    for hh, (p, vnb, s_h) in enumerate(zip(pr, vnbs, s_old)):
        s_ref[hh] = s_h * p["gl"] + _dot_tn(p["kd"], vnb)
    for hh, o in enumerate(os_):
        cs = slice(hh * DH, (hh + 1) * DH)
        on = o * lax.rsqrt(jnp.mean(o * o, axis=-1, keepdims=True) + EPS) * gnw
        zc = z_ref[:, cs].astype(F32)
        o_ref[:, cs] = (on * _silu(zc)).astype(o_ref.dtype)


def _gdn(p3, small, conv_w, a_log, dt_bias, gdn_norm_w):
    b, t, _ = p3.shape
    hb, rows = GDN_HB, GDN_CH
    g = HEADS // hb
    hbw = hb * DH
    nblk = HW // hbw
    bl = small[..., 0:HEADS].reshape(b, t, g, hb)
    al = small[..., HEADS:2 * HEADS].reshape(b, t, g, hb)
    scc = jnp.transpose(jnp.concatenate([bl, al], axis=-1), (0, 2, 1, 3))
    scr = jnp.transpose(scc, (0, 1, 3, 2))
    hpr = jnp.stack([a_log.reshape(g, hb), dt_bias.reshape(g, hb)], axis=1).astype(F32)
    hpc = jnp.transpose(hpr, (0, 2, 1))
    cw = conv_w.astype(F32)

    def col(off):
        return lambda bi, gi, ti: (bi, ti, off * nblk + gi)

    def halo(off):
        return lambda bi, gi, ti: (bi, jnp.maximum(ti * (rows // GDN_HALO) - 1, 0), off * nblk + gi)

    def cwmap(off):
        return lambda bi, gi, ti: (0, off * nblk + gi)

    blk = lambda off: pl.BlockSpec((None, rows, hbw), col(off))
    hblk = lambda off: pl.BlockSpec((None, GDN_HALO, hbw), halo(off))
    in_specs = [blk(0), blk(1), blk(2), blk(3), hblk(0), hblk(1), hblk(2),
                pl.BlockSpec((CONV_K, hbw), cwmap(0)),
                pl.BlockSpec((CONV_K, hbw), cwmap(1)),
                pl.BlockSpec((CONV_K, hbw), cwmap(2)),
                pl.BlockSpec((None, None, rows, 2 * hb), lambda bi, gi, ti: (bi, gi, ti, 0)),
                pl.BlockSpec((None, None, 2 * hb, rows), lambda bi, gi, ti: (bi, gi, 0, ti)),
                pl.BlockSpec((None, 2, hb), lambda bi, gi, ti: (gi, 0, 0)),
                pl.BlockSpec((None, hb, 2), lambda bi, gi, ti: (gi, 0, 0)),
                pl.BlockSpec((1, DH), lambda bi, gi, ti: (0, 0))]
    return pl.pallas_call(
        functools.partial(_gdn_kernel, hb=hb, rows=rows),
        grid=(b, g, t // rows),
        in_specs=in_specs,
        out_specs=pl.BlockSpec((None, rows, hbw), lambda bi, gi, ti: (bi, ti, gi)),
        out_shape=jax.ShapeDtypeStruct((b, t, HW), BF16),
        scratch_shapes=[pltpu.VMEM((hb, DH, DH), F32)],
        compiler_params=pltpu.CompilerParams(
            dimension_semantics=("parallel", "parallel", "arbitrary"),
            vmem_limit_bytes=_vmem_limit(8 << 20)),
        name="gdn",
    )(p3, p3, p3, p3, p3, p3, p3, cw, cw, cw, scc, scr, hpr, hpc,
      gdn_norm_w.reshape(1, DH).astype(F32))


FOX_PREP_ROWS = 512
FOX_KW = 2 * DH


def _split3(x):
    hi = x.astype(BF16)
    r1 = x - hi.astype(F32)
    mid = r1.astype(BF16)
    lo = (r1 - mid.astype(F32)).astype(BF16)
    return hi, mid, lo


def _fox_prep_kernel(q_ref, k_ref, sm_ref, bf_ref, wq_ref, wk_ref,
                     qn_ref, kn_ref, carry_ref, *, rows):
    t = pl.program_id(1)

    @pl.when(t == 0)
    def _():
        carry_ref[...] = jnp.zeros_like(carry_ref)

    x = sm_ref[:, 2 * HEADS:3 * HEADS] + bf_ref[...]
    ls = -_softplus(-x)
    ri = lax.broadcasted_iota(jnp.int32, (rows, rows), 0)
    ci = lax.broadcasted_iota(jnp.int32, (rows, rows), 1)
    tril = (ri >= ci).astype(BF16)
    hi, mid, lo = _split3(ls)
    c = _dot(tril, hi) + (_dot(tril, mid) + _dot(tril, lo)) + carry_ref[...]
    carry_ref[...] = c[rows - 1:rows, :]
    nh, nm, nl = (piece.astype(F32) for piece in _split3(-LOG2E * c))

    wq = wq_ref[...] * (DH ** -0.5 * LOG2E)
    wk = wk_ref[...]
    lane = lax.broadcasted_iota(jnp.int32, (rows, DH), 1)
    for h in range(HEADS):
        cs = slice(h * DH, (h + 1) * DH)
        q = q_ref[:, cs].astype(F32)
        k = k_ref[:, cs].astype(F32)
        qn_ref[:, cs] = (q * lax.rsqrt(jnp.mean(q * q, axis=-1, keepdims=True) + EPS) * wq).astype(qn_ref.dtype)
        kn_ref[:, h * FOX_KW:h * FOX_KW + DH] = (
            k * lax.rsqrt(jnp.mean(k * k, axis=-1, keepdims=True) + EPS) * wk).astype(kn_ref.dtype)
        extra = jnp.where(lane == 0, nh[:, h:h + 1],
                          jnp.where(lane == 1, nm[:, h:h + 1], jnp.where(lane == 2, nl[:, h:h + 1], 0.0)))
        kn_ref[:, h * FOX_KW + DH:(h + 1) * FOX_KW] = extra.astype(kn_ref.dtype)


def _fox_prep(p3, small3, fox_b_f, wq, wk):
    b, t, _ = p3.shape
    rows = _pick(t, (FOX_PREP_ROWS, 256, 128))
    return pl.pallas_call(
        functools.partial(_fox_prep_kernel, rows=rows),
        grid=(b, t // rows),
        in_specs=[pl.BlockSpec((None, rows, HW), lambda bi, ti: (bi, ti, 0)),
                  pl.BlockSpec((None, rows, HW), lambda bi, ti: (bi, ti, 1)),
                  pl.BlockSpec((None, rows, LANES), lambda bi, ti: (bi, ti, 0)),
                  pl.BlockSpec((1, HEADS), lambda bi, ti: (0, 0)),
                  pl.BlockSpec((1, DH), lambda bi, ti: (0, 0)),
                  pl.BlockSpec((1, DH), lambda bi, ti: (0, 0))],
        out_specs=[pl.BlockSpec((None, rows, HW), lambda bi, ti: (bi, ti, 0)),
                   pl.BlockSpec((None, rows, HEADS * FOX_KW), lambda bi, ti: (bi, ti, 0))],
        out_shape=[jax.ShapeDtypeStruct((b, t, HW), BF16),
                   jax.ShapeDtypeStruct((b, t, HEADS * FOX_KW), BF16)],
        scratch_shapes=[pltpu.VMEM((1, HEADS), F32)],
        compiler_params=pltpu.CompilerParams(
            dimension_semantics=("parallel", "arbitrary"),
            vmem_limit_bytes=_vmem_limit(24 << 20)),
        name="fox_prep",
    )(p3, p3, small3, fox_b_f.reshape(1, HEADS).astype(F32),
      wq.reshape(1, DH).astype(F32), wk.reshape(1, DH).astype(F32))


FOX_BQ = 2048
FOX_SUB = 256


def _fox_kernel(q_ref, k_ref, v_ref, o_ref, vp_ref, *, bq, sub):
    qi = pl.program_id(2)
    nsub = bq // sub

    @pl.when(qi == 0)
    def _():
        vp_ref[:, 0:DH] = v_ref[...]
        lane_v = lax.broadcasted_iota(jnp.int32, (vp_ref.shape[0], DH), 1)
        vp_ref[:, DH:FOX_KW] = jnp.where(lane_v == 0, 1.0, 0.0).astype(BF16)

    lane_q = lax.broadcasted_iota(jnp.int32, (sub, DH), 1)
    ones3 = jnp.where(lane_q < 3, 1.0, 0.0).astype(BF16)
    qs = [jnp.concatenate([q_ref[i * sub:(i + 1) * sub, :], ones3], axis=1) for i in range(nsub)]

    def step(carry, off, widths, masks):
        ss = [_dot_nt(qs[i], k_ref[pl.ds(off, widths[i]), :]) for i in range(nsub)]
        if masks is not None:
            ss = [jnp.where(masks[i], ss[i], -jnp.inf) for i in range(nsub)]
        ms = [jnp.maximum(carry[i][0], jnp.max(ss[i], axis=-1, keepdims=True)) for i in range(nsub)]
        ps = [jnp.exp2(ss[i] - ms[i]).astype(BF16) for i in range(nsub)]
        pvs = [_dot(ps[i], vp_ref[pl.ds(off, widths[i]), :]) for i in range(nsub)]
        return tuple((ms[i], jnp.exp2(carry[i][0] - ms[i]) * carry[i][1] + pvs[i]) for i in range(nsub))

    def body(j, carry):
        return step(carry, pl.multiple_of(j * bq, bq), [bq] * nsub, None)

    init = tuple((jnp.full((sub, 1), -jnp.inf, F32), jnp.zeros((sub, FOX_KW), F32)) for _ in range(nsub))
    carry = lax.fori_loop(0, qi, body, init)

    widths = [(i + 1) * sub for i in range(nsub)]
    masks = [(lax.broadcasted_iota(jnp.int32, (sub, w), 0) + i * sub) >= lax.broadcasted_iota(jnp.int32, (sub, w), 1)
             for i, w in enumerate(widths)]
    carry = step(carry, pl.multiple_of(qi * bq, bq), widths, masks)
    for i in range(nsub):
        acc = carry[i][1]
        o_ref[i * sub:(i + 1) * sub, :] = (acc[:, 0:DH] * (1.0 / acc[:, DH:DH + 1])).astype(o_ref.dtype)


def _fox(qn, kn, p3):
    b, t, _ = qn.shape
    bq = _pick(t, (FOX_BQ, 256, 128))
    sub = min(FOX_SUB, bq)
    nq = t // bq
    voff = 2 * HW // DH
    return pl.pallas_call(
        functools.partial(_fox_kernel, bq=bq, sub=sub),
        grid=(b, HEADS, nq),
        in_specs=[pl.BlockSpec((None, bq, DH), lambda bi, hi, qi: (bi, qi, hi)),
                  pl.BlockSpec((None, t, FOX_KW), lambda bi, hi, qi: (bi, 0, hi)),
                  pl.BlockSpec((None, t, DH), lambda bi, hi, qi: (bi, 0, voff + hi))],
        out_specs=pl.BlockSpec((None, bq, DH), lambda bi, hi, qi: (bi, qi, hi)),
        out_shape=jax.ShapeDtypeStruct((b, t, HW), BF16),
        scratch_shapes=[pltpu.VMEM((t, FOX_KW), BF16)],
        compiler_params=pltpu.CompilerParams(
            dimension_semantics=("parallel", "parallel", "arbitrary"),
            vmem_limit_bytes=_vmem_limit(40 << 20)),
        name="fox",
    )(qn, kn, p3)


def _merge_kernel(oa_ref, ob_ref, wa_ref, wb_ref, ga_ref, gb_ref, o_ref):
    ya = _dot(oa_ref[...], wa_ref[...])
    yb = _dot(ob_ref[...], wb_ref[...])
    o_ref[...] = (_sigmoid(ga_ref[...].astype(F32)) * ya
                  + _sigmoid(gb_ref[...].astype(F32)) * yb).astype(o_ref.dtype)


def _merge(oa, ob, wa, wb, gates):
    m, k = oa.shape
    n = wa.shape[1]
    bm = _pick(m, (1024, 512, 256, 128, 64, 8))
    bn = _pick(n, (1024, 512, 256, 128))
    gb0 = n // bn
    nbytes = 2 * (2 * bm * k * 2 + 2 * k * bn * 2 + 3 * bm * bn * 2)
    return pl.pallas_call(
        _merge_kernel,
        grid=(m // bm, n // bn),
        in_specs=[pl.BlockSpec((bm, k), lambda i, j: (i, 0)),
                  pl.BlockSpec((bm, k), lambda i, j: (i, 0)),
                  pl.BlockSpec((k, bn), lambda i, j: (0, j)),
                  pl.BlockSpec((k, bn), lambda i, j: (0, j)),
                  pl.BlockSpec((bm, bn), lambda i, j: (i, j)),
                  pl.BlockSpec((bm, bn), lambda i, j: (i, gb0 + j))],
        out_specs=pl.BlockSpec((bm, bn), lambda i, j: (i, j)),
        out_shape=jax.ShapeDtypeStruct((m, n), BF16),
        compiler_params=pltpu.CompilerParams(
            dimension_semantics=("parallel", "parallel"),
            vmem_limit_bytes=_vmem_limit(nbytes)),
        name="merge",
    )(oa, ob, wa, wb, gates, gates)


def _ffn1_kernel(x_ref, wg_ref, wu_ref, o_ref):
    x = x_ref[...]
    g = _dot(x, wg_ref[...])
    u = _dot(x, wu_ref[...])
    o_ref[...] = (_silu(g) * u).astype(o_ref.dtype)


def _ffn1(hn, wg, wu):
    m, k = hn.shape
    n = wg.shape[1]
    bm = _pick(m, (2048, 1024, 512, 256, 128, 64, 8))
    bn = _pick(n, (512, 256, 128))
    nbytes = 2 * (bm * k * 2 + 2 * k * bn * 2 + bm * bn * 2) + 2 * bm * bn * 4
    return pl.pallas_call(
        _ffn1_kernel,
        grid=(m // bm, n // bn),
        in_specs=[pl.BlockSpec((bm, k), lambda i, j: (i, 0)),
                  pl.BlockSpec((k, bn), lambda i, j: (0, j)),
                  pl.BlockSpec((k, bn), lambda i, j: (0, j))],
        out_specs=pl.BlockSpec((bm, bn), lambda i, j: (i, j)),
        out_shape=jax.ShapeDtypeStruct((m, n), BF16),
        compiler_params=pltpu.CompilerParams(
            dimension_semantics=("parallel", "parallel"),
            vmem_limit_bytes=_vmem_limit(nbytes)),
        name="ffn_up",
    )(hn, wg, wu)


def _layer(x, norm_mix_w, w_in, conv_w, a_log, dt_bias, gdn_norm_w, fox_b_f,
           fox_q_norm_w, fox_k_norm_w, w_branch_a, w_branch_b, w_out,
           norm_ffn_w, w_ffn_gate, w_ffn_up, w_ffn_down):
    b, t, d = x.shape
    m = b * t
    x2 = x.reshape(m, d)

    o_small_a = 4 * HW
    o_b = o_small_a + 2 * HEADS
    o_f = o_b + 3 * HW
    o_gate = o_f + HEADS
    w_t = w_in.T.astype(BF16)
    wt_small = jnp.concatenate([w_t[o_small_a:o_b], w_t[o_f:o_gate]], axis=0)
    wt_small = jnp.pad(wt_small, ((0, LANES - 3 * HEADS), (0, 0)))

    xn = _rmsnorm(x2, norm_mix_w, "norm_mix")
    p_gdn = _matmul_nt(xn, w_t, 0, o_small_a, BF16, "in_proj_gdn").reshape(b, t, 4 * HW)
    p_fox = _matmul_nt(xn, w_t, o_b, o_f - o_b, BF16, "in_proj_fox").reshape(b, t, 3 * HW)
    p_gate = _matmul_nt(xn, w_t, o_gate, w_t.shape[0] - o_gate, BF16, "in_proj_gate")
    small3 = _matmul_nt(xn, wt_small, 0, LANES, F32, "in_proj_small").reshape(b, t, LANES)

    o_a = _gdn(p_gdn, small3, conv_w, a_log, dt_bias, gdn_norm_w)
    qn, kn = _fox_prep(p_fox, small3, fox_b_f, fox_q_norm_w, fox_k_norm_w)
    o_b_ = _fox(qn, kn, p_fox)

    merged = _merge(o_a.reshape(m, HW), o_b_.reshape(m, HW),
                    w_branch_a.astype(BF16), w_branch_b.astype(BF16), p_gate)
    h = _matmul(merged, w_out.astype(BF16), F32, "out_proj", residual=x2)

    hn = _rmsnorm(h, norm_ffn_w, "norm_ffn")
    act = _ffn1(hn, w_ffn_gate.astype(BF16), w_ffn_up.astype(BF16))
    out = _matmul(act, w_ffn_down.astype(BF16), F32, "ffn_down", residual=h,
                  bm_prefs=(512, 256, 128, 64, 8), bn_prefs=(512, 256, 128))
    return out.reshape(b, t, d)


def kernel(x, norm_mix_w, w_in, conv_w, a_log, dt_bias, gdn_norm_w, fox_b_f, fox_q_norm_w, fox_k_norm_w, w_branch_a, w_branch_b, w_out, norm_ffn_w, w_ffn_gate, w_ffn_up, w_ffn_down):
    h = x
    for l in range(norm_mix_w.shape[0]):
        h = _layer(h, norm_mix_w[l], w_in[l], conv_w[l], a_log[l], dt_bias[l], gdn_norm_w[l],
                   fox_b_f[l], fox_q_norm_w[l], fox_k_norm_w[l], w_branch_a[l], w_branch_b[l],
                   w_out[l], norm_ffn_w[l], w_ffn_gate[l], w_ffn_up[l], w_ffn_down[l])
    return h
```

```python
import functools

import jax
import jax.numpy as jnp
from jax import lax
from jax.experimental import pallas as pl
from jax.experimental.pallas import tpu as pltpu

F32 = jnp.float32
BF16 = jnp.bfloat16

EPS = 1e-6
HEADS = 16
DH = 128
HW = HEADS * DH
CONV_K = 4

V7X_VMEM_BYTES = 64 * 1024 * 1024
LANES = 128
SUBLANES = 8
BF16_SUBLANES = 16
LOG2E = 1.4426950408889634


def _vmem_limit(nbytes):
    return int(min(nbytes + (16 << 20), V7X_VMEM_BYTES - (4 << 20)))


def _pick(n, prefs):
    for p in prefs:
        if n % p == 0:
            return p
    return n


def _sigmoid(x):
    return 1.0 / (1.0 + jnp.exp(-x))


def _silu(x):
    return x * _sigmoid(x)


def _softplus(x):
    return jnp.maximum(x, 0.0) + jnp.log(1.0 + jnp.exp(-jnp.abs(x)))


def _dot(a, b):
    return jnp.dot(a, b, preferred_element_type=F32)


def _dot_nt(a, b):
    return lax.dot_general(a, b, (((1,), (1,)), ((), ())), preferred_element_type=F32)


def _dot_tn(a, b):
    return lax.dot_general(a, b, (((0,), (0,)), ((), ())), preferred_element_type=F32)


def _with_side_casts(body, n_in, n_out, n_side):
    def wrapped(*refs):
        ins, refs = refs[:n_in], refs[n_in:]
        side_in, refs = refs[:n_side], refs[n_side:]
        outs, refs = refs[:n_out], refs[n_out:]
        side_out, scratch = refs[:n_side], refs[n_side:]
        for si, so in zip(side_in, side_out):
            so[...] = si[...].astype(so.dtype)
        body(*ins, *outs, *scratch)
    return wrapped


def _side_ok(w, nsteps):
    return w.shape[0] % nsteps == 0 and (w.shape[0] // nsteps) % BF16_SUBLANES == 0


def _split_sides(sides, nsteps):
    ok = [w for w in sides if _side_ok(w, nsteps)]

    def finish(casts):
        it = iter(casts)
        return [next(it) if _side_ok(w, nsteps) else w.astype(BF16) for w in sides]
    return ok, finish


def _side_specs(sides, nsteps, step_of):
    specs, shapes, nbytes = [], [], 0
    for w in sides:
        r = w.shape[0] // nsteps
        specs.append(pl.BlockSpec((r, w.shape[1]), lambda *g: (step_of(*g), 0)))
        shapes.append(jax.ShapeDtypeStruct(w.shape, BF16))
        nbytes += 2 * r * w.shape[1] * 6
    return specs, list(specs), shapes, nbytes


def _rmsnorm_kernel(x_ref, w_ref, o_ref):
    x = x_ref[...]
    ms = jnp.mean(x * x, axis=-1, keepdims=True)
    o_ref[...] = (x * lax.rsqrt(ms + EPS) * w_ref[...]).astype(o_ref.dtype)


def _rmsnorm(x, w, name):
    m, d = x.shape
    bm = _pick(m, (256, 128, 64, 8))
    return pl.pallas_call(
        _rmsnorm_kernel,
        grid=(m // bm,),
        in_specs=[pl.BlockSpec((bm, d), lambda i: (i, 0)),
                  pl.BlockSpec((1, d), lambda i: (0, 0))],
        out_specs=pl.BlockSpec((bm, d), lambda i: (i, 0)),
        out_shape=jax.ShapeDtypeStruct((m, d), BF16),
        compiler_params=pltpu.CompilerParams(
            dimension_semantics=("parallel",),
            vmem_limit_bytes=_vmem_limit(2 * bm * d * 6)),
        name=name,
    )(x, w.reshape(1, d).astype(F32))


def _mm_nt_kernel(x_ref, wt_ref, o_ref):
    o_ref[...] = _dot_nt(x_ref[...], wt_ref[...]).astype(o_ref.dtype)


def _matmul_nt(x, wt, row0, n, out_dtype, name, sides=()):
    m, k = x.shape
    bm = _pick(m, (1024, 512, 256, 128, 64, 8))
    bn = _pick(n, (1024, 512, 256, 128))
    nj = n // bn
    sides, finish = _split_sides(sides, (m // bm) * nj)
    s_in, s_out, s_shapes, s_bytes = _side_specs(sides, (m // bm) * nj, lambda i, j: i * nj + j)
    nbytes = 2 * (bm * k * 2 + k * bn * 2 + bm * bn * jnp.dtype(out_dtype).itemsize) + s_bytes
    res = pl.pallas_call(
        _with_side_casts(_mm_nt_kernel, 2, 1, len(sides)),
        grid=(m // bm, nj),
        in_specs=[pl.BlockSpec((bm, k), lambda i, j: (i, 0)),
                  pl.BlockSpec((pl.Element(bn), pl.Element(k)),
                               lambda i, j: (pl.multiple_of(row0 + j * bn, BF16_SUBLANES), 0))] + s_in,
        out_specs=[pl.BlockSpec((bm, bn), lambda i, j: (i, j))] + s_out,
        out_shape=[jax.ShapeDtypeStruct((m, n), out_dtype)] + s_shapes,
        compiler_params=pltpu.CompilerParams(
            dimension_semantics=("parallel", "parallel"),
            vmem_limit_bytes=_vmem_limit(nbytes)),
        name=name,
    )(x, wt, *sides)
    return [res[0], *finish(res[1:])]


def _mm_kernel(x_ref, w_ref, o_ref):
    o_ref[...] = _dot(x_ref[...], w_ref[...]).astype(o_ref.dtype)


def _mm_res_kernel(x_ref, w_ref, r_ref, o_ref):
    o_ref[...] = (r_ref[...] + _dot(x_ref[...], w_ref[...])).astype(o_ref.dtype)


def _matmul(x, w, out_dtype, name, residual=None, bm_prefs=(1024, 512, 256, 128, 64, 8),
            bn_prefs=(1024, 512, 256, 128)):
    m, k = x.shape
    _, n = w.shape
    bm = _pick(m, bm_prefs)
    bn = _pick(n, bn_prefs)
    osz = jnp.dtype(out_dtype).itemsize
    nbytes = 2 * (bm * k * 2 + k * bn * 2 + bm * bn * osz)
    in_specs = [pl.BlockSpec((bm, k), lambda i, j: (i, 0)),
                pl.BlockSpec((k, bn), lambda i, j: (0, j))]
    args = [x, w]
    kern = _mm_kernel
    if residual is not None:
        in_specs.append(pl.BlockSpec((bm, bn), lambda i, j: (i, j)))
        args.append(residual)
        kern = _mm_res_kernel
        nbytes += 2 * bm * bn * 4
    return pl.pallas_call(
        kern,
        grid=(m // bm, n // bn),
        in_specs=in_specs,
        out_specs=pl.BlockSpec((bm, bn), lambda i, j: (i, j)),
        out_shape=jax.ShapeDtypeStruct((m, n), out_dtype),
        compiler_params=pltpu.CompilerParams(
            dimension_semantics=("parallel", "parallel"),
            vmem_limit_bytes=_vmem_limit(nbytes)),
        name=name,
    )(*args)


GDN_CH = 128
GDN_HB = 16
GDN_HALO = 16


def _gdn_kernel(q_ref, k_ref, v_ref, z_ref, qh_ref, kh_ref, vh_ref,
                cwq_ref, cwk_ref, cwv_ref, scc_ref, scr_ref, hpr_ref, hpc_ref, gnw_ref,
                o_ref, s_ref, *, hb, rows):
    t = pl.program_id(2)

    @pl.when(t == 0)
    def _():
        s_ref[...] = jnp.zeros_like(s_ref)

    sr = lax.broadcasted_iota(jnp.int32, (3 * rows, rows + GDN_HALO), 0)
    sc = lax.broadcasted_iota(jnp.int32, (3 * rows, rows + GDN_HALO), 1)
    shift = (sc == (sr % rows) + (sr // rows) + (GDN_HALO - 3)).astype(BF16)

    def conv_silu(x_ref, h_ref, cw_ref):
        x = x_ref[...]
        halo = jnp.where(t > 0, h_ref[...], jnp.zeros_like(h_ref))
        sh = _dot(shift, jnp.concatenate([halo, x], axis=0))
        cw = cw_ref[...]
        y = x.astype(F32) * cw[CONV_K - 1:CONV_K, :]
        for i in range(CONV_K - 1):
            y = y + sh[i * rows:(i + 1) * rows] * cw[i:i + 1, :]
        return _silu(y)

    q_all = conv_silu(q_ref, qh_ref, cwq_ref)
    k_all = conv_silu(k_ref, kh_ref, cwk_ref)
    v_all = conv_silu(v_ref, vh_ref, cwv_ref)

    scc = scc_ref[...]
    scr = scr_ref[...]
    hpr = hpr_ref[...]
    hpc = hpc_ref[...]
    beta_c = _sigmoid(scc[:, 0:hb])
    g_c = -jnp.exp(hpr[0:1, :]) * _softplus(scc[:, hb:2 * hb] + hpr[1:2, :])
    g_r = -jnp.exp(hpc[:, 0:1]) * _softplus(scr[hb:2 * hb, :] + hpc[:, 1:2])

    ri = lax.broadcasted_iota(jnp.int32, (rows, rows), 0)
    ci = lax.broadcasted_iota(jnp.int32, (rows, rows), 1)
    tril_incl = ri >= ci
    tril_strict = ri > ci
    triu_incl = ri <= ci
    eye = (ri == ci).astype(F32)
    masks = []
    s = 1
    while s < rows:
        masks.append(((ri // (2 * s)) == (ci // (2 * s))) & (((ri // s) % 2) == 1) & (((ci // s) % 2) == 0))
        s *= 2
    gnw = gnw_ref[...]

    pr = []
    for hh in range(hb):
        cs = slice(hh * DH, (hh + 1) * DH)
        qc = q_all[:, cs]
        kc = k_all[:, cs]
        vc = v_all[:, cs]
        qc = qc * (lax.rsqrt(jnp.sum(qc * qc, axis=-1, keepdims=True) + EPS) * (DH ** -0.5))
        kc = kc * lax.rsqrt(jnp.sum(kc * kc, axis=-1, keepdims=True) + EPS)
        beta = beta_c[:, hh:hh + 1]
        gcol = jnp.sum(jnp.where(tril_incl, g_r[hh:hh + 1, :], 0.0), axis=1, keepdims=True)
        grow = jnp.sum(jnp.where(triu_incl, g_c[:, hh:hh + 1], 0.0), axis=0, keepdims=True)
        decay = jnp.exp(jnp.where(tril_incl, gcol - grow, -jnp.inf))
        kb = kc.astype(BF16)
        qkk = _dot_nt(jnp.concatenate([qc.astype(BF16), kb], axis=0), kb)
        qk = (qkk[0:rows] * decay).astype(BF16)
        a_strict = jnp.where(tril_strict, qkk[rows:2 * rows] * (decay * beta), 0.0)
        egc = jnp.exp(gcol)
        g_last = gcol[rows - 1:rows, :]
        pr.append(dict(
            a=a_strict.astype(BF16),
            t=eye - jnp.where(masks[0], a_strict, 0.0),
            rhs=jnp.concatenate([vc * beta, kc * (beta * egc)], axis=1).astype(BF16),
            qe=(qc * egc).astype(BF16),
            kd=(kc * jnp.exp(g_last - gcol)).astype(BF16),
            qk=qk, gl=jnp.exp(g_last)))

    for mk in masks[1:]:
        tbs = [p["t"].astype(BF16) for p in pr]
        xs = [_dot(tb, p["a"]).astype(BF16) for tb, p in zip(tbs, pr)]
        ys = [_dot(x, tb) for x, tb in zip(xs, tbs)]
        for p, y in zip(pr, ys):
            p["t"] = p["t"] - jnp.where(mk, y, 0.0)

    uws = [_dot(p["t"].astype(BF16), p["rhs"]) for p in pr]

    s_old = [s_ref[hh] for hh in range(hb)]
    wss = [_dot(jnp.concatenate([uw[:, DH:2 * DH].astype(BF16), p["qe"]], axis=0), s_h.astype(BF16))
           for uw, p, s_h in zip(uws, pr, s_old)]
    vnbs = [(uw[:, 0:DH] - ws[0:rows]).astype(BF16) for uw, ws in zip(uws, wss)]
    os_ = [ws[rows:2 * rows] + _dot(p["qk"], vnb) for ws, p, vnb in zip(wss, pr, vnbs)]
    for hh, (p, vnb, s_h) in enumerate(zip(pr, vnbs, s_old)):
        s_ref[hh] = s_h * p["gl"] + _dot_tn(p["kd"], vnb)
    for hh, o in enumerate(os_):
        cs = slice(hh * DH, (hh + 1) * DH)
        on = o * lax.rsqrt(jnp.mean(o * o, axis=-1, keepdims=True) + EPS) * gnw
        zc = z_ref[:, cs].astype(F32)
        o_ref[:, cs] = (on * _silu(zc)).astype(o_ref.dtype)


def _gdn(p3, small, conv_w, a_log, dt_bias, gdn_norm_w, sides=()):
    b, t, _ = p3.shape
    hb, rows = GDN_HB, GDN_CH
    g = HEADS // hb
    hbw = hb * DH
    nblk = HW // hbw
    bl = small[..., 0:HEADS].reshape(b, t, g, hb)
    al = small[..., HEADS:2 * HEADS].reshape(b, t, g, hb)
    scc = jnp.transpose(jnp.concatenate([bl, al], axis=-1), (0, 2, 1, 3))
    scr = jnp.transpose(scc, (0, 1, 3, 2))
    hpr = jnp.stack([a_log.reshape(g, hb), dt_bias.reshape(g, hb)], axis=1).astype(F32)
    hpc = jnp.transpose(hpr, (0, 2, 1))
    cw = conv_w.astype(F32)

    def col(off):
        return lambda bi, gi, ti: (bi, ti, off * nblk + gi)

    def halo(off):
        return lambda bi, gi, ti: (bi, jnp.maximum(ti * (rows // GDN_HALO) - 1, 0), off * nblk + gi)

    def cwmap(off):
        return lambda bi, gi, ti: (0, off * nblk + gi)

    blk = lambda off: pl.BlockSpec((None, rows, hbw), col(off))
    hblk = lambda off: pl.BlockSpec((None, GDN_HALO, hbw), halo(off))
    in_specs = [blk(0), blk(1), blk(2), blk(3), hblk(0), hblk(1), hblk(2),
                pl.BlockSpec((CONV_K, hbw), cwmap(0)),
                pl.BlockSpec((CONV_K, hbw), cwmap(1)),
                pl.BlockSpec((CONV_K, hbw), cwmap(2)),
                pl.BlockSpec((None, None, rows, 2 * hb), lambda bi, gi, ti: (bi, gi, ti, 0)),
                pl.BlockSpec((None, None, 2 * hb, rows), lambda bi, gi, ti: (bi, gi, 0, ti)),
                pl.BlockSpec((None, 2, hb), lambda bi, gi, ti: (gi, 0, 0)),
                pl.BlockSpec((None, hb, 2), lambda bi, gi, ti: (gi, 0, 0)),
                pl.BlockSpec((1, DH), lambda bi, gi, ti: (0, 0))]
    nt = t // rows
    sides, finish = _split_sides(sides, b * g * nt)
    s_in, s_out, s_shapes, s_bytes = _side_specs(sides, b * g * nt, lambda bi, gi, ti: (bi * g + gi) * nt + ti)
    res = pl.pallas_call(
        _with_side_casts(functools.partial(_gdn_kernel, hb=hb, rows=rows), len(in_specs), 1, len(sides)),
        grid=(b, g, nt),
        in_specs=in_specs + s_in,
        out_specs=[pl.BlockSpec((None, rows, hbw), lambda bi, gi, ti: (bi, ti, gi))] + s_out,
        out_shape=[jax.ShapeDtypeStruct((b, t, HW), BF16)] + s_shapes,
        scratch_shapes=[pltpu.VMEM((hb, DH, DH), F32)],
        compiler_params=pltpu.CompilerParams(
            dimension_semantics=("parallel", "parallel", "arbitrary"),
            vmem_limit_bytes=_vmem_limit((8 << 20) + s_bytes)),
        name="gdn",
    )(p3, p3, p3, p3, p3, p3, p3, cw, cw, cw, scc, scr, hpr, hpc,
      gdn_norm_w.reshape(1, DH).astype(F32), *sides)
    return [res[0], *finish(res[1:])]


FOX_PREP_ROWS = 512
FOX_KW = 2 * DH


def _split3(x):
    hi = x.astype(BF16)
    r1 = x - hi.astype(F32)
    mid = r1.astype(BF16)
    lo = (r1 - mid.astype(F32)).astype(BF16)
    return hi, mid, lo


def _fox_prep_kernel(q_ref, k_ref, sm_ref, bf_ref, wq_ref, wk_ref,
                     qn_ref, kn_ref, carry_ref, *, rows):
    t = pl.program_id(1)

    @pl.when(t == 0)
    def _():
        carry_ref[...] = jnp.zeros_like(carry_ref)

    x = sm_ref[:, 2 * HEADS:3 * HEADS] + bf_ref[...]
    ls = -_softplus(-x)
    ri = lax.broadcasted_iota(jnp.int32, (rows, rows), 0)
    ci = lax.broadcasted_iota(jnp.int32, (rows, rows), 1)
    tril = (ri >= ci).astype(BF16)
    hi, mid, lo = _split3(ls)
    c = _dot(tril, hi) + (_dot(tril, mid) + _dot(tril, lo)) + carry_ref[...]
    carry_ref[...] = c[rows - 1:rows, :]
    nh, nm, nl = (piece.astype(F32) for piece in _split3(-LOG2E * c))

    wq = wq_ref[...] * (DH ** -0.5 * LOG2E)
    wk = wk_ref[...]
    lane = lax.broadcasted_iota(jnp.int32, (rows, DH), 1)
    for h in range(HEADS):
        cs = slice(h * DH, (h + 1) * DH)
        q = q_ref[:, cs].astype(F32)
        k = k_ref[:, cs].astype(F32)
        qn_ref[:, cs] = (q * lax.rsqrt(jnp.mean(q * q, axis=-1, keepdims=True) + EPS) * wq).astype(qn_ref.dtype)
        kn_ref[:, h * FOX_KW:h * FOX_KW + DH] = (
            k * lax.rsqrt(jnp.mean(k * k, axis=-1, keepdims=True) + EPS) * wk).astype(kn_ref.dtype)
        extra = jnp.where(lane == 0, nh[:, h:h + 1],
                          jnp.where(lane == 1, nm[:, h:h + 1], jnp.where(lane == 2, nl[:, h:h + 1], 0.0)))
        kn_ref[:, h * FOX_KW + DH:(h + 1) * FOX_KW] = extra.astype(kn_ref.dtype)


def _fox_prep(p3, small3, fox_b_f, wq, wk):
    b, t, _ = p3.shape
    rows = _pick(t, (FOX_PREP_ROWS, 256, 128))
    return pl.pallas_call(
        functools.partial(_fox_prep_kernel, rows=rows),
        grid=(b, t // rows),
        in_specs=[pl.BlockSpec((None, rows, HW), lambda bi, ti: (bi, ti, 0)),
                  pl.BlockSpec((None, rows, HW), lambda bi, ti: (bi, ti, 1)),
                  pl.BlockSpec((None, rows, LANES), lambda bi, ti: (bi, ti, 0)),
                  pl.BlockSpec((1, HEADS), lambda bi, ti: (0, 0)),
                  pl.BlockSpec((1, DH), lambda bi, ti: (0, 0)),
                  pl.BlockSpec((1, DH), lambda bi, ti: (0, 0))],
        out_specs=[pl.BlockSpec((None, rows, HW), lambda bi, ti: (bi, ti, 0)),
                   pl.BlockSpec((None, rows, HEADS * FOX_KW), lambda bi, ti: (bi, ti, 0))],
        out_shape=[jax.ShapeDtypeStruct((b, t, HW), BF16),
                   jax.ShapeDtypeStruct((b, t, HEADS * FOX_KW), BF16)],
        scratch_shapes=[pltpu.VMEM((1, HEADS), F32)],
        compiler_params=pltpu.CompilerParams(
            dimension_semantics=("parallel", "arbitrary"),
            vmem_limit_bytes=_vmem_limit(24 << 20)),
        name="fox_prep",
    )(p3, p3, small3, fox_b_f.reshape(1, HEADS).astype(F32),
      wq.reshape(1, DH).astype(F32), wk.reshape(1, DH).astype(F32))


FOX_BQ = 2048
FOX_SUB = 256


def _fox_kernel(q_ref, k_ref, v_ref, o_ref, vp_ref, *, bq, sub):
    qi = pl.program_id(2)
    nsub = bq // sub

    @pl.when(qi == 0)
    def _():
        vp_ref[:, 0:DH] = v_ref[...]
        lane_v = lax.broadcasted_iota(jnp.int32, (vp_ref.shape[0], DH), 1)
        vp_ref[:, DH:FOX_KW] = jnp.where(lane_v == 0, 1.0, 0.0).astype(BF16)

    lane_q = lax.broadcasted_iota(jnp.int32, (sub, DH), 1)
    ones3 = jnp.where(lane_q < 3, 1.0, 0.0).astype(BF16)
    qs = [jnp.concatenate([q_ref[i * sub:(i + 1) * sub, :], ones3], axis=1) for i in range(nsub)]

    def step(carry, off, widths, masks):
        ss = [_dot_nt(qs[i], k_ref[pl.ds(off, widths[i]), :]) for i in range(nsub)]
        if masks is not None:
            ss = [jnp.where(masks[i], ss[i], -jnp.inf) for i in range(nsub)]
        ms = [jnp.maximum(carry[i][0], jnp.max(ss[i], axis=-1, keepdims=True)) for i in range(nsub)]
        ps = [jnp.exp2(ss[i] - ms[i]).astype(BF16) for i in range(nsub)]
        pvs = [_dot(ps[i], vp_ref[pl.ds(off, widths[i]), :]) for i in range(nsub)]
        return tuple((ms[i], jnp.exp2(carry[i][0] - ms[i]) * carry[i][1] + pvs[i]) for i in range(nsub))

    def body(j, carry):
        return step(carry, pl.multiple_of(j * bq, bq), [bq] * nsub, None)

    init = tuple((jnp.full((sub, 1), -jnp.inf, F32), jnp.zeros((sub, FOX_KW), F32)) for _ in range(nsub))
    carry = lax.fori_loop(0, qi, body, init)

    widths = [(i + 1) * sub for i in range(nsub)]
    masks = [(lax.broadcasted_iota(jnp.int32, (sub, w), 0) + i * sub) >= lax.broadcasted_iota(jnp.int32, (sub, w), 1)
             for i, w in enumerate(widths)]
    carry = step(carry, pl.multiple_of(qi * bq, bq), widths, masks)
    for i in range(nsub):
        acc = carry[i][1]
        o_ref[i * sub:(i + 1) * sub, :] = (acc[:, 0:DH] * (1.0 / acc[:, DH:DH + 1])).astype(o_ref.dtype)


def _fox(qn, kn, p3, sides=()):
    b, t, _ = qn.shape
    bq = _pick(t, (FOX_BQ, 256, 128))
    sub = min(FOX_SUB, bq)
    nq = t // bq
    voff = 2 * HW // DH
    sides, finish = _split_sides(sides, b * HEADS * nq)
    s_in, s_out, s_shapes, s_bytes = _side_specs(sides, b * HEADS * nq, lambda bi, hi, qi: (bi * HEADS + hi) * nq + qi)
    res = pl.pallas_call(
        _with_side_casts(functools.partial(_fox_kernel, bq=bq, sub=sub), 3, 1, len(sides)),
        grid=(b, HEADS, nq),
        in_specs=[pl.BlockSpec((None, bq, DH), lambda bi, hi, qi: (bi, qi, hi)),
                  pl.BlockSpec((None, t, FOX_KW), lambda bi, hi, qi: (bi, 0, hi)),
                  pl.BlockSpec((None, t, DH), lambda bi, hi, qi: (bi, 0, voff + hi))] + s_in,
        out_specs=[pl.BlockSpec((None, bq, DH), lambda bi, hi, qi: (bi, qi, hi))] + s_out,
        out_shape=[jax.ShapeDtypeStruct((b, t, HW), BF16)] + s_shapes,
        scratch_shapes=[pltpu.VMEM((t, FOX_KW), BF16)],
        compiler_params=pltpu.CompilerParams(
            dimension_semantics=("parallel", "parallel", "arbitrary"),
            vmem_limit_bytes=_vmem_limit((40 << 20) + s_bytes)),
        name="fox",
    )(qn, kn, p3, *sides)
    return [res[0], *finish(res[1:])]


def _merge_kernel(oa_ref, ob_ref, wa_ref, wb_ref, ga_ref, gb_ref, o_ref):
    ya = _dot(oa_ref[...], wa_ref[...])
    yb = _dot(ob_ref[...], wb_ref[...])
    o_ref[...] = (_sigmoid(ga_ref[...].astype(F32)) * ya
                  + _sigmoid(gb_ref[...].astype(F32)) * yb).astype(o_ref.dtype)


def _merge(oa, ob, wa, wb, gates):
    m, k = oa.shape
    n = wa.shape[1]
    bm = _pick(m, (1024, 512, 256, 128, 64, 8))
    bn = _pick(n, (1024, 512, 256, 128))
    gb0 = n // bn
    nbytes = 2 * (2 * bm * k * 2 + 2 * k * bn * 2 + 3 * bm * bn * 2)
    return pl.pallas_call(
        _merge_kernel,
        grid=(m // bm, n // bn),
        in_specs=[pl.BlockSpec((bm, k), lambda i, j: (i, 0)),
                  pl.BlockSpec((bm, k), lambda i, j: (i, 0)),
                  pl.BlockSpec((k, bn), lambda i, j: (0, j)),
                  pl.BlockSpec((k, bn), lambda i, j: (0, j)),
                  pl.BlockSpec((bm, bn), lambda i, j: (i, j)),
                  pl.BlockSpec((bm, bn), lambda i, j: (i, gb0 + j))],
        out_specs=pl.BlockSpec((bm, bn), lambda i, j: (i, j)),
        out_shape=jax.ShapeDtypeStruct((m, n), BF16),
        compiler_params=pltpu.CompilerParams(
            dimension_semantics=("parallel", "parallel"),
            vmem_limit_bytes=_vmem_limit(nbytes)),
        name="merge",
    )(oa, ob, wa, wb, gates, gates)


def _ffn1_kernel(x_ref, wg_ref, wu_ref, o_ref):
    x = x_ref[...]
    g = _dot(x, wg_ref[...])
    u = _dot(x, wu_ref[...])
    o_ref[...] = (_silu(g) * u).astype(o_ref.dtype)


def _ffn1(hn, wg, wu, sides=()):
    m, k = hn.shape
    n = wg.shape[1]
    bm = _pick(m, (2048, 1024, 512, 256, 128, 64, 8))
    bn = _pick(n, (512, 256, 128))
    nj = n // bn
    sides, finish = _split_sides(sides, (m // bm) * nj)
    s_in, s_out, s_shapes, s_bytes = _side_specs(sides, (m // bm) * nj, lambda i, j: i * nj + j)
    nbytes = 2 * (bm * k * 2 + 2 * k * bn * 2 + bm * bn * 2) + 2 * bm * bn * 4 + s_bytes
    res = pl.pallas_call(
        _with_side_casts(_ffn1_kernel, 3, 1, len(sides)),
        grid=(m // bm, nj),
        in_specs=[pl.BlockSpec((bm, k), lambda i, j: (i, 0)),
                  pl.BlockSpec((k, bn), lambda i, j: (0, j)),
                  pl.BlockSpec((k, bn), lambda i, j: (0, j))] + s_in,
        out_specs=[pl.BlockSpec((bm, bn), lambda i, j: (i, j))] + s_out,
        out_shape=[jax.ShapeDtypeStruct((m, n), BF16)] + s_shapes,
        compiler_params=pltpu.CompilerParams(
            dimension_semantics=("parallel", "parallel"),
            vmem_limit_bytes=_vmem_limit(nbytes)),
        name="ffn_up",
    )(hn, wg, wu, *sides)
    return [res[0], *finish(res[1:])]


def _layer(x, norm_mix_w, w_in, conv_w, a_log, dt_bias, gdn_norm_w, fox_b_f,
           fox_q_norm_w, fox_k_norm_w, w_branch_a, w_branch_b, w_out,
           norm_ffn_w, w_ffn_gate, w_ffn_up, w_ffn_down):
    b, t, d = x.shape
    m = b * t
    x2 = x.reshape(m, d)

    o_small_a = 4 * HW
    o_b = o_small_a + 2 * HEADS
    o_f = o_b + 3 * HW
    o_gate = o_f + HEADS
    w_t = w_in.T.astype(BF16)
    wt_small = jnp.concatenate([w_t[o_small_a:o_b], w_t[o_f:o_gate]], axis=0)
    wt_small = jnp.pad(wt_small, ((0, LANES - 3 * HEADS), (0, 0)))

    xn = _rmsnorm(x2, norm_mix_w, "norm_mix")
    p_gdn, wg = _matmul_nt(xn, w_t, 0, o_small_a, BF16, "in_proj_gdn", sides=(w_ffn_gate,))
    p_fox, = _matmul_nt(xn, w_t, o_b, o_f - o_b, BF16, "in_proj_fox")
    p_gate, wu = _matmul_nt(xn, w_t, o_gate, w_t.shape[0] - o_gate, BF16, "in_proj_gate", sides=(w_ffn_up,))
    small, = _matmul_nt(xn, wt_small, 0, LANES, F32, "in_proj_small")
    p_gdn = p_gdn.reshape(b, t, 4 * HW)
    p_fox = p_fox.reshape(b, t, 3 * HW)
    small3 = small.reshape(b, t, LANES)

    o_a, wa, wb = _gdn(p_gdn, small3, conv_w, a_log, dt_bias, gdn_norm_w, sides=(w_branch_a, w_branch_b))
    qn, kn = _fox_prep(p_fox, small3, fox_b_f, fox_q_norm_w, fox_k_norm_w)
    o_b_, wo = _fox(qn, kn, p_fox, sides=(w_out,))

    merged = _merge(o_a.reshape(m, HW), o_b_.reshape(m, HW), wa, wb, p_gate)
    h = _matmul(merged, wo, F32, "out_proj", residual=x2)

    hn = _rmsnorm(h, norm_ffn_w, "norm_ffn")
    act, wd = _ffn1(hn, wg, wu, sides=(w_ffn_down,))
    out = _matmul(act, wd, F32, "ffn_down", residual=h,
                  bm_prefs=(512, 256, 128, 64, 8), bn_prefs=(512, 256, 128))
    return out.reshape(b, t, d)


def kernel(x, norm_mix_w, w_in, conv_w, a_log, dt_bias, gdn_norm_w, fox_b_f, fox_q_norm_w, fox_k_norm_w, w_branch_a, w_branch_b, w_out, norm_ffn_w, w_ffn_gate, w_ffn_up, w_ffn_down):
    h = x
    for l in range(norm_mix_w.shape[0]):
        h = _layer(h, norm_mix_w[l], w_in[l], conv_w[l], a_log[l], dt_bias[l], gdn_norm_w[l],
                   fox_b_f[l], fox_q_norm_w[l], fox_k_norm_w[l], w_branch_a[l], w_branch_b[l],
                   w_out[l], norm_ffn_w[l], w_ffn_gate[l], w_ffn_up[l], w_ffn_down[l])
    return h
```

```python
import functools

import jax
import jax.numpy as jnp
from jax import lax
from jax.experimental import pallas as pl
from jax.experimental.pallas import tpu as pltpu

F32 = jnp.float32
BF16 = jnp.bfloat16

EPS = 1e-6
HEADS = 16
DH = 128
HW = HEADS * DH
CONV_K = 4

V7X_VMEM_BYTES = 64 * 1024 * 1024
LANES = 128
SUBLANES = 8
BF16_SUBLANES = 16
LOG2E = 1.4426950408889634


def _vmem_limit(nbytes):
    return int(min(nbytes + (16 << 20), V7X_VMEM_BYTES - (4 << 20)))


def _pick(n, prefs):
    for p in prefs:
        if n % p == 0:
            return p
    return n


def _sigmoid(x):
    return 1.0 / (1.0 + jnp.exp(-x))


def _silu(x):
    return x * _sigmoid(x)


def _softplus(x):
    return jnp.maximum(x, 0.0) + jnp.log(1.0 + jnp.exp(-jnp.abs(x)))


def _dot(a, b):
    return jnp.dot(a, b, preferred_element_type=F32)


def _dot_nt(a, b):
    return lax.dot_general(a, b, (((1,), (1,)), ((), ())), preferred_element_type=F32)


def _dot_tn(a, b):
    return lax.dot_general(a, b, (((0,), (0,)), ((), ())), preferred_element_type=F32)


def _with_side_casts(body, n_in, n_out, n_side):
    def wrapped(*refs):
        ins, refs = refs[:n_in], refs[n_in:]
        side_in, refs = refs[:n_side], refs[n_side:]
        outs, refs = refs[:n_out], refs[n_out:]
        side_out, scratch = refs[:n_side], refs[n_side:]
        for si, so in zip(side_in, side_out):
            so[...] = si[...].astype(so.dtype)
        body(*ins, *outs, *scratch)
    return wrapped


def _side_window(side):
    if isinstance(side, tuple):
        return side
    return side, 0, side.shape[0]


def _side_ok(side, nsteps):
    _, _, nrows = _side_window(side)
    return nrows % nsteps == 0 and (nrows // nsteps) % BF16_SUBLANES == 0


def _split_sides(sides, nsteps):
    ok = [_side_window(s) for s in sides if _side_ok(s, nsteps)]

    def finish(casts):
        it = iter(casts)
        out = []
        for s in sides:
            w, row0, nrows = _side_window(s)
            out.append(next(it) if _side_ok(s, nsteps) else w[row0:row0 + nrows].astype(BF16))
        return out
    return [w for w, _, _ in ok], [(row0, nrows) for _, row0, nrows in ok], finish


def _side_specs(arrays, windows, nsteps, step_of):
    in_specs, out_specs, shapes, nbytes = [], [], [], 0
    for w, (row0, nrows) in zip(arrays, windows):
        r, c = nrows // nsteps, w.shape[1]
        in_specs.append(pl.BlockSpec(
            (pl.Element(r), pl.Element(c)),
            lambda *g, row0=row0, r=r: (pl.multiple_of(row0 + step_of(*g) * r, SUBLANES), 0)))
        out_specs.append(pl.BlockSpec((r, c), lambda *g: (step_of(*g), 0)))
        shapes.append(jax.ShapeDtypeStruct((nrows, c), BF16))
        nbytes += 2 * r * c * 6
    return in_specs, out_specs, shapes, nbytes


def _cast_kernel(*refs):
    n = len(refs) // 2
    for si, so in zip(refs[:n], refs[n:]):
        so[...] = si[...].astype(so.dtype)


def _cast_rows(w, main, extras, name):
    c = w.shape[1]
    row0, nrows = main
    r = _pick(nrows, (512, 256, 128, 64, 32, 16))
    in_specs = [pl.BlockSpec((pl.Element(r), pl.Element(c)),
                             lambda i: (pl.multiple_of(row0 + i * r, SUBLANES), 0))]
    out_specs = [pl.BlockSpec((r, c), lambda i: (i, 0))]
    shapes = [jax.ShapeDtypeStruct((nrows, c), BF16)]
    for e0, en in extras:
        in_specs.append(pl.BlockSpec((pl.Element(en), pl.Element(c)), lambda i, e0=e0: (e0, 0)))
        out_specs.append(pl.BlockSpec((en, c), lambda i: (0, 0)))
        shapes.append(jax.ShapeDtypeStruct((en, c), BF16))
    return pl.pallas_call(
        _cast_kernel,
        grid=(nrows // r,),
        in_specs=in_specs,
        out_specs=out_specs,
        out_shape=shapes,
        compiler_params=pltpu.CompilerParams(
            dimension_semantics=("arbitrary",),
            vmem_limit_bytes=_vmem_limit(2 * (r + sum(en for _, en in extras)) * c * 6)),
        name=name,
    )(*([w] * (1 + len(extras))))


def _rmsnorm_kernel(x_ref, w_ref, o_ref):
    x = x_ref[...]
    ms = jnp.mean(x * x, axis=-1, keepdims=True)
    o_ref[...] = (x * lax.rsqrt(ms + EPS) * w_ref[...]).astype(o_ref.dtype)


def _rmsnorm(x, w, name):
    m, d = x.shape
    bm = _pick(m, (512, 256, 128, 64, 8))
    return pl.pallas_call(
        _rmsnorm_kernel,
        grid=(m // bm,),
        in_specs=[pl.BlockSpec((bm, d), lambda i: (i, 0)),
                  pl.BlockSpec((1, d), lambda i: (0, 0))],
        out_specs=pl.BlockSpec((bm, d), lambda i: (i, 0)),
        out_shape=jax.ShapeDtypeStruct((m, d), BF16),
        compiler_params=pltpu.CompilerParams(
            dimension_semantics=("parallel",),
            vmem_limit_bytes=_vmem_limit(2 * bm * d * 6)),
        name=name,
    )(x, w.reshape(1, d).astype(F32))


def _mm_nt_kernel(x_ref, wt_ref, o_ref):
    o_ref[...] = _dot_nt(x_ref[...], wt_ref[...]).astype(o_ref.dtype)


def _matmul_nt(x, wt, row0, n, out_dtype, name, sides=()):
    m, k = x.shape
    bm = _pick(m, (1024, 512, 256, 128, 64, 8))
    bn = _pick(n, (1024, 512, 256, 128))
    nj = n // bn
    sides, wins, finish = _split_sides(sides, (m // bm) * nj)
    s_in, s_out, s_shapes, s_bytes = _side_specs(sides, wins, (m // bm) * nj, lambda i, j: i * nj + j)
    nbytes = 2 * (bm * k * 2 + k * bn * 2 + bm * bn * jnp.dtype(out_dtype).itemsize) + s_bytes
    res = pl.pallas_call(
        _with_side_casts(_mm_nt_kernel, 2, 1, len(sides)),
        grid=(m // bm, nj),
        in_specs=[pl.BlockSpec((bm, k), lambda i, j: (i, 0)),
                  pl.BlockSpec((pl.Element(bn), pl.Element(k)),
                               lambda i, j: (pl.multiple_of(row0 + j * bn, BF16_SUBLANES), 0))] + s_in,
        out_specs=[pl.BlockSpec((bm, bn), lambda i, j: (i, j))] + s_out,
        out_shape=[jax.ShapeDtypeStruct((m, n), out_dtype)] + s_shapes,
        compiler_params=pltpu.CompilerParams(
            dimension_semantics=("parallel", "parallel"),
            vmem_limit_bytes=_vmem_limit(nbytes)),
        name=name,
    )(x, wt, *sides)
    return [res[0], *finish(res[1:])]


def _mm_kernel(x_ref, w_ref, o_ref):
    o_ref[...] = _dot(x_ref[...], w_ref[...]).astype(o_ref.dtype)


def _mm_res_kernel(x_ref, w_ref, r_ref, o_ref):
    o_ref[...] = (r_ref[...] + _dot(x_ref[...], w_ref[...])).astype(o_ref.dtype)


def _matmul(x, w, out_dtype, name, residual=None, bm_prefs=(1024, 512, 256, 128, 64, 8),
            bn_prefs=(1024, 512, 256, 128)):
    m, k = x.shape
    _, n = w.shape
    bm = _pick(m, bm_prefs)
    bn = _pick(n, bn_prefs)
    osz = jnp.dtype(out_dtype).itemsize
    nbytes = 2 * (bm * k * 2 + k * bn * 2 + bm * bn * osz)
    in_specs = [pl.BlockSpec((bm, k), lambda i, j: (i, 0)),
                pl.BlockSpec((k, bn), lambda i, j: (0, j))]
    args = [x, w]
    kern = _mm_kernel
    if residual is not None:
        in_specs.append(pl.BlockSpec((bm, bn), lambda i, j: (i, j)))
        args.append(residual)
        kern = _mm_res_kernel
        nbytes += 2 * bm * bn * 4
    return pl.pallas_call(
        kern,
        grid=(m // bm, n // bn),
        in_specs=in_specs,
        out_specs=pl.BlockSpec((bm, bn), lambda i, j: (i, j)),
        out_shape=jax.ShapeDtypeStruct((m, n), out_dtype),
        compiler_params=pltpu.CompilerParams(
            dimension_semantics=("parallel", "parallel"),
            vmem_limit_bytes=_vmem_limit(nbytes)),
        name=name,
    )(*args)


GDN_CH = 128
GDN_HB = 16
GDN_HALO = 16


def _gdn_kernel(q_ref, k_ref, v_ref, z_ref, qh_ref, kh_ref, vh_ref,
                cwq_ref, cwk_ref, cwv_ref, scc_ref, scr_ref, hpr_ref, hpc_ref, gnw_ref,
                o_ref, s_ref, *, hb, rows):
    t = pl.program_id(2)

    @pl.when(t == 0)
    def _():
        s_ref[...] = jnp.zeros_like(s_ref)

    sr = lax.broadcasted_iota(jnp.int32, (3 * rows, rows + GDN_HALO), 0)
    sc = lax.broadcasted_iota(jnp.int32, (3 * rows, rows + GDN_HALO), 1)
    shift = (sc == (sr % rows) + (sr // rows) + (GDN_HALO - 3)).astype(BF16)

    def conv_silu(x_ref, h_ref, cw_ref):
        x = x_ref[...]
        halo = jnp.where(t > 0, h_ref[...], jnp.zeros_like(h_ref))
        sh = _dot(shift, jnp.concatenate([halo, x], axis=0))
        cw = cw_ref[...]
        y = x.astype(F32) * cw[CONV_K - 1:CONV_K, :]
        for i in range(CONV_K - 1):
            y = y + sh[i * rows:(i + 1) * rows] * cw[i:i + 1, :]
        return _silu(y)

    q_all = conv_silu(q_ref, qh_ref, cwq_ref)
    k_all = conv_silu(k_ref, kh_ref, cwk_ref)
    v_all = conv_silu(v_ref, vh_ref, cwv_ref)

    scc = scc_ref[...]
    scr = scr_ref[...]
    hpr = hpr_ref[...]
    hpc = hpc_ref[...]
    beta_c = _sigmoid(scc[:, 0:hb])
    g_c = -jnp.exp(hpr[0:1, :]) * _softplus(scc[:, hb:2 * hb] + hpr[1:2, :])
    g_r = -jnp.exp(hpc[:, 0:1]) * _softplus(scr[hb:2 * hb, :] + hpc[:, 1:2])

    ri = lax.broadcasted_iota(jnp.int32, (rows, rows), 0)
    ci = lax.broadcasted_iota(jnp.int32, (rows, rows), 1)
    tril_incl = ri >= ci
    tril_strict = ri > ci
    triu_incl = ri <= ci
    eye = (ri == ci).astype(F32)
    masks = []
    s = 1
    while s < rows:
        masks.append(((ri // (2 * s)) == (ci // (2 * s))) & (((ri // s) % 2) == 1) & (((ci // s) % 2) == 0))
        s *= 2
    gnw = gnw_ref[...]

    pr = []
    for hh in range(hb):
        cs = slice(hh * DH, (hh + 1) * DH)
        qc = q_all[:, cs]
        kc = k_all[:, cs]
        vc = v_all[:, cs]
        qc = qc * (lax.rsqrt(jnp.sum(qc * qc, axis=-1, keepdims=True) + EPS) * (DH ** -0.5))
        kc = kc * lax.rsqrt(jnp.sum(kc * kc, axis=-1, keepdims=True) + EPS)
        beta = beta_c[:, hh:hh + 1]
        gcol = jnp.sum(jnp.where(tril_incl, g_r[hh:hh + 1, :], 0.0), axis=1, keepdims=True)
        grow = jnp.sum(jnp.where(triu_incl, g_c[:, hh:hh + 1], 0.0), axis=0, keepdims=True)
        decay = jnp.exp(jnp.where(tril_incl, gcol - grow, -jnp.inf))
        kb = kc.astype(BF16)
        qkk = _dot_nt(jnp.concatenate([qc.astype(BF16), kb], axis=0), kb)
        qk = (qkk[0:rows] * decay).astype(BF16)
        a_strict = jnp.where(tril_strict, qkk[rows:2 * rows] * (decay * beta), 0.0)
        egc = jnp.exp(gcol)
        g_last = gcol[rows - 1:rows, :]
        pr.append(dict(
            a=a_strict.astype(BF16),
            t=eye - jnp.where(masks[0], a_strict, 0.0),
            rhs=jnp.concatenate([vc * beta, kc * (beta * egc)], axis=1).astype(BF16),
            qe=(qc * egc).astype(BF16),
            kd=(kc * jnp.exp(g_last - gcol)).astype(BF16),
            qk=qk, gl=jnp.exp(g_last)))

    for mk in masks[1:]:
        tbs = [p["t"].astype(BF16) for p in pr]
        xs = [_dot(tb, p["a"]).astype(BF16) for tb, p in zip(tbs, pr)]
        ys = [_dot(x, tb) for x, tb in zip(xs, tbs)]
        for p, y in zip(pr, ys):
            p["t"] = p["t"] - jnp.where(mk, y, 0.0)

    uws = [_dot(p["t"].astype(BF16), p["rhs"]) for p in pr]

    s_old = [s_ref[hh] for hh in range(hb)]
    wss = [_dot(jnp.concatenate([uw[:, DH:2 * DH].astype(BF16), p["qe"]], axis=0), s_h.astype(BF16))
           for uw, p, s_h in zip(uws, pr, s_old)]
    vnbs = [(uw[:, 0:DH] - ws[0:rows]).astype(BF16) for uw, ws in zip(uws, wss)]
    os_ = [ws[rows:2 * rows] + _dot(p["qk"], vnb) for ws, p, vnb in zip(wss, pr, vnbs)]
    for hh, (p, vnb, s_h) in enumerate(zip(pr, vnbs, s_old)):
        s_ref[hh] = s_h * p["gl"] + _dot_tn(p["kd"], vnb)
    for hh, o in enumerate(os_):
        cs = slice(hh * DH, (hh + 1) * DH)
        on = o * lax.rsqrt(jnp.mean(o * o, axis=-1, keepdims=True) + EPS) * gnw
        zc = z_ref[:, cs].astype(F32)
        o_ref[:, cs] = (on * _silu(zc)).astype(o_ref.dtype)


def _gdn(p3, small, conv_w, a_log, dt_bias, gdn_norm_w, sides=()):
    b, t, _ = p3.shape
    hb, rows = GDN_HB, GDN_CH
    g = HEADS // hb
    hbw = hb * DH
    nblk = HW // hbw
    bl = small[..., 0:HEADS].reshape(b, t, g, hb)
    al = small[..., HEADS:2 * HEADS].reshape(b, t, g, hb)
    scc = jnp.transpose(jnp.concatenate([bl, al], axis=-1), (0, 2, 1, 3))
    scr = jnp.transpose(scc, (0, 1, 3, 2))
    hpr = jnp.stack([a_log.reshape(g, hb), dt_bias.reshape(g, hb)], axis=1).astype(F32)
    hpc = jnp.transpose(hpr, (0, 2, 1))
    cw = conv_w.astype(F32)

    def col(off):
        return lambda bi, gi, ti: (bi, ti, off * nblk + gi)

    def halo(off):
        return lambda bi, gi, ti: (bi, jnp.maximum(ti * (rows // GDN_HALO) - 1, 0), off * nblk + gi)

    def cwmap(off):
        return lambda bi, gi, ti: (0, off * nblk + gi)

    blk = lambda off: pl.BlockSpec((None, rows, hbw), col(off))
    hblk = lambda off: pl.BlockSpec((None, GDN_HALO, hbw), halo(off))
    in_specs = [blk(0), blk(1), blk(2), blk(3), hblk(0), hblk(1), hblk(2),
                pl.BlockSpec((CONV_K, hbw), cwmap(0)),
                pl.BlockSpec((CONV_K, hbw), cwmap(1)),
                pl.BlockSpec((CONV_K, hbw), cwmap(2)),
                pl.BlockSpec((None, None, rows, 2 * hb), lambda bi, gi, ti: (bi, gi, ti, 0)),
                pl.BlockSpec((None, None, 2 * hb, rows), lambda bi, gi, ti: (bi, gi, 0, ti)),
                pl.BlockSpec((None, 2, hb), lambda bi, gi, ti: (gi, 0, 0)),
                pl.BlockSpec((None, hb, 2), lambda bi, gi, ti: (gi, 0, 0)),
                pl.BlockSpec((1, DH), lambda bi, gi, ti: (0, 0))]
    nt = t // rows
    sides, wins, finish = _split_sides(sides, b * g * nt)
    s_in, s_out, s_shapes, s_bytes = _side_specs(sides, wins, b * g * nt, lambda bi, gi, ti: (bi * g + gi) * nt + ti)
    res = pl.pallas_call(
        _with_side_casts(functools.partial(_gdn_kernel, hb=hb, rows=rows), len(in_specs), 1, len(sides)),
        grid=(b, g, nt),
        in_specs=in_specs + s_in,
        out_specs=[pl.BlockSpec((None, rows, hbw), lambda bi, gi, ti: (bi, ti, gi))] + s_out,
        out_shape=[jax.ShapeDtypeStruct((b, t, HW), BF16)] + s_shapes,
        scratch_shapes=[pltpu.VMEM((hb, DH, DH), F32)],
        compiler_params=pltpu.CompilerParams(
            dimension_semantics=("parallel", "parallel", "arbitrary"),
            vmem_limit_bytes=_vmem_limit((8 << 20) + s_bytes)),
        name="gdn",
    )(p3, p3, p3, p3, p3, p3, p3, cw, cw, cw, scc, scr, hpr, hpc,
      gdn_norm_w.reshape(1, DH).astype(F32), *sides)
    return [res[0], *finish(res[1:])]


FOX_PREP_ROWS = 512
FOX_KW = 2 * DH


def _split3(x):
    hi = x.astype(BF16)
    r1 = x - hi.astype(F32)
    mid = r1.astype(BF16)
    lo = (r1 - mid.astype(F32)).astype(BF16)
    return hi, mid, lo


def _fox_prep_kernel(q_ref, k_ref, sm_ref, bf_ref, wq_ref, wk_ref,
                     qn_ref, kn_ref, carry_ref, *, rows):
    t = pl.program_id(1)

    @pl.when(t == 0)
    def _():
        carry_ref[...] = jnp.zeros_like(carry_ref)

    x = sm_ref[:, 2 * HEADS:3 * HEADS] + bf_ref[...]
    ls = -_softplus(-x)
    ri = lax.broadcasted_iota(jnp.int32, (rows, rows), 0)
    ci = lax.broadcasted_iota(jnp.int32, (rows, rows), 1)
    tril = (ri >= ci).astype(BF16)
    hi, mid, lo = _split3(ls)
    c = _dot(tril, hi) + (_dot(tril, mid) + _dot(tril, lo)) + carry_ref[...]
    carry_ref[...] = c[rows - 1:rows, :]
    nh, nm, nl = (piece.astype(F32) for piece in _split3(-LOG2E * c))

    wq = wq_ref[...] * (DH ** -0.5 * LOG2E)
    wk = wk_ref[...]
    lane = lax.broadcasted_iota(jnp.int32, (rows, DH), 1)
    for h in range(HEADS):
        cs = slice(h * DH, (h + 1) * DH)
        q = q_ref[:, cs].astype(F32)
        k = k_ref[:, cs].astype(F32)
        qn_ref[:, cs] = (q * lax.rsqrt(jnp.mean(q * q, axis=-1, keepdims=True) + EPS) * wq).astype(qn_ref.dtype)
        kn_ref[:, h * FOX_KW:h * FOX_KW + DH] = (
            k * lax.rsqrt(jnp.mean(k * k, axis=-1, keepdims=True) + EPS) * wk).astype(kn_ref.dtype)
        extra = jnp.where(lane == 0, nh[:, h:h + 1],
                          jnp.where(lane == 1, nm[:, h:h + 1], jnp.where(lane == 2, nl[:, h:h + 1], 0.0)))
        kn_ref[:, h * FOX_KW + DH:(h + 1) * FOX_KW] = extra.astype(kn_ref.dtype)


def _fox_prep(p3, small3, fox_b_f, wq, wk):
    b, t, _ = p3.shape
    rows = _pick(t, (FOX_PREP_ROWS, 256, 128))
    return pl.pallas_call(
        functools.partial(_fox_prep_kernel, rows=rows),
        grid=(b, t // rows),
        in_specs=[pl.BlockSpec((None, rows, HW), lambda bi, ti: (bi, ti, 0)),
                  pl.BlockSpec((None, rows, HW), lambda bi, ti: (bi, ti, 1)),
                  pl.BlockSpec((None, rows, LANES), lambda bi, ti: (bi, ti, 0)),
                  pl.BlockSpec((1, HEADS), lambda bi, ti: (0, 0)),
                  pl.BlockSpec((1, DH), lambda bi, ti: (0, 0)),
                  pl.BlockSpec((1, DH), lambda bi, ti: (0, 0))],
        out_specs=[pl.BlockSpec((None, rows, HW), lambda bi, ti: (bi, ti, 0)),
                   pl.BlockSpec((None, rows, HEADS * FOX_KW), lambda bi, ti: (bi, ti, 0))],
        out_shape=[jax.ShapeDtypeStruct((b, t, HW), BF16),
                   jax.ShapeDtypeStruct((b, t, HEADS * FOX_KW), BF16)],
        scratch_shapes=[pltpu.VMEM((1, HEADS), F32)],
        compiler_params=pltpu.CompilerParams(
            dimension_semantics=("parallel", "arbitrary"),
            vmem_limit_bytes=_vmem_limit(24 << 20)),
        name="fox_prep",
    )(p3, p3, small3, fox_b_f.reshape(1, HEADS).astype(F32),
      wq.reshape(1, DH).astype(F32), wk.reshape(1, DH).astype(F32))


FOX_BQ = 2048
FOX_SUB = 256


def _fox_kernel(q_ref, k_ref, v_ref, o_ref, vp_ref, *, bq, sub):
    qi = pl.program_id(2)
    nsub = bq // sub

    @pl.when(qi == 0)
    def _():
        vp_ref[:, 0:DH] = v_ref[...]
        lane_v = lax.broadcasted_iota(jnp.int32, (vp_ref.shape[0], DH), 1)
        vp_ref[:, DH:FOX_KW] = jnp.where(lane_v == 0, 1.0, 0.0).astype(BF16)

    lane_q = lax.broadcasted_iota(jnp.int32, (sub, DH), 1)
    ones3 = jnp.where(lane_q < 3, 1.0, 0.0).astype(BF16)
    qs = [jnp.concatenate([q_ref[i * sub:(i + 1) * sub, :], ones3], axis=1) for i in range(nsub)]

    def step(carry, off, widths, masks):
        ss = [_dot_nt(qs[i], k_ref[pl.ds(off, widths[i]), :]) for i in range(nsub)]
        if masks is not None:
            ss = [jnp.where(masks[i], ss[i], -jnp.inf) for i in range(nsub)]
        ms = [jnp.maximum(carry[i][0], jnp.max(ss[i], axis=-1, keepdims=True)) for i in range(nsub)]
        ps = [jnp.exp2(ss[i] - ms[i]).astype(BF16) for i in range(nsub)]
        pvs = [_dot(ps[i], vp_ref[pl.ds(off, widths[i]), :]) for i in range(nsub)]
        return tuple((ms[i], jnp.exp2(carry[i][0] - ms[i]) * carry[i][1] + pvs[i]) for i in range(nsub))

    def body(j, carry):
        return step(carry, pl.multiple_of(j * bq, bq), [bq] * nsub, None)

    init = tuple((jnp.full((sub, 1), -jnp.inf, F32), jnp.zeros((sub, FOX_KW), F32)) for _ in range(nsub))
    carry = lax.fori_loop(0, qi, body, init)

    widths = [(i + 1) * sub for i in range(nsub)]
    masks = [(lax.broadcasted_iota(jnp.int32, (sub, w), 0) + i * sub) >= lax.broadcasted_iota(jnp.int32, (sub, w), 1)
             for i, w in enumerate(widths)]
    carry = step(carry, pl.multiple_of(qi * bq, bq), widths, masks)
    for i in range(nsub):
        acc = carry[i][1]
        o_ref[i * sub:(i + 1) * sub, :] = (acc[:, 0:DH] * (1.0 / acc[:, DH:DH + 1])).astype(o_ref.dtype)


def _fox(qn, kn, p3, sides=()):
    b, t, _ = qn.shape
    bq = _pick(t, (FOX_BQ, 256, 128))
    sub = min(FOX_SUB, bq)
    nq = t // bq
    voff = 2 * HW // DH
    sides, wins, finish = _split_sides(sides, b * HEADS * nq)
    s_in, s_out, s_shapes, s_bytes = _side_specs(sides, wins, b * HEADS * nq,
                                                 lambda bi, hi, qi: (bi * HEADS + hi) * nq + qi)
    res = pl.pallas_call(
        _with_side_casts(functools.partial(_fox_kernel, bq=bq, sub=sub), 3, 1, len(sides)),
        grid=(b, HEADS, nq),
        in_specs=[pl.BlockSpec((None, bq, DH), lambda bi, hi, qi: (bi, qi, hi)),
                  pl.BlockSpec((None, t, FOX_KW), lambda bi, hi, qi: (bi, 0, hi)),
                  pl.BlockSpec((None, t, DH), lambda bi, hi, qi: (bi, 0, voff + hi))] + s_in,
        out_specs=[pl.BlockSpec((None, bq, DH), lambda bi, hi, qi: (bi, qi, hi))] + s_out,
        out_shape=[jax.ShapeDtypeStruct((b, t, HW), BF16)] + s_shapes,
        scratch_shapes=[pltpu.VMEM((t, FOX_KW), BF16)],
        compiler_params=pltpu.CompilerParams(
            dimension_semantics=("parallel", "parallel", "arbitrary"),
            vmem_limit_bytes=_vmem_limit((40 << 20) + s_bytes)),
        name="fox",
    )(qn, kn, p3, *sides)
    return [res[0], *finish(res[1:])]


def _merge_kernel(oa_ref, ob_ref, wa_ref, wb_ref, ga_ref, gb_ref, o_ref):
    ya = _dot(oa_ref[...], wa_ref[...])
    yb = _dot(ob_ref[...], wb_ref[...])
    o_ref[...] = (_sigmoid(ga_ref[...].astype(F32)) * ya
                  + _sigmoid(gb_ref[...].astype(F32)) * yb).astype(o_ref.dtype)


def _merge(oa, ob, wa, wb, gates):
    m, k = oa.shape
    n = wa.shape[1]
    bm = _pick(m, (1024, 512, 256, 128, 64, 8))
    bn = _pick(n, (1024, 512, 256, 128))
    gb0 = n // bn
    nbytes = 2 * (2 * bm * k * 2 + 2 * k * bn * 2 + 3 * bm * bn * 2)
    return pl.pallas_call(
        _merge_kernel,
        grid=(m // bm, n // bn),
        in_specs=[pl.BlockSpec((bm, k), lambda i, j: (i, 0)),
                  pl.BlockSpec((bm, k), lambda i, j: (i, 0)),
                  pl.BlockSpec((k, bn), lambda i, j: (0, j)),
                  pl.BlockSpec((k, bn), lambda i, j: (0, j)),
                  pl.BlockSpec((bm, bn), lambda i, j: (i, j)),
                  pl.BlockSpec((bm, bn), lambda i, j: (i, gb0 + j))],
        out_specs=pl.BlockSpec((bm, bn), lambda i, j: (i, j)),
        out_shape=jax.ShapeDtypeStruct((m, n), BF16),
        compiler_params=pltpu.CompilerParams(
            dimension_semantics=("parallel", "parallel"),
            vmem_limit_bytes=_vmem_limit(nbytes)),
        name="merge",
    )(oa, ob, wa, wb, gates, gates)


def _ffn1_kernel(x_ref, wg_ref, wu_ref, o_ref):
    x = x_ref[...]
    g = _dot(x, wg_ref[...])
    u = _dot(x, wu_ref[...])
    o_ref[...] = (_silu(g) * u).astype(o_ref.dtype)


def _ffn1(hn, wg, wu, sides=()):
    m, k = hn.shape
    n = wg.shape[1]
    bm = _pick(m, (2048, 1024, 512, 256, 128, 64, 8))
    bn = _pick(n, (512, 256, 128))
    nj = n // bn
    sides, wins, finish = _split_sides(sides, (m // bm) * nj)
    s_in, s_out, s_shapes, s_bytes = _side_specs(sides, wins, (m // bm) * nj, lambda i, j: i * nj + j)
    nbytes = 2 * (bm * k * 2 + 2 * k * bn * 2 + bm * bn * 2) + 2 * bm * bn * 4 + s_bytes
    res = pl.pallas_call(
        _with_side_casts(_ffn1_kernel, 3, 1, len(sides)),
        grid=(m // bm, nj),
        in_specs=[pl.BlockSpec((bm, k), lambda i, j: (i, 0)),
                  pl.BlockSpec((k, bn), lambda i, j: (0, j)),
                  pl.BlockSpec((k, bn), lambda i, j: (0, j))] + s_in,
        out_specs=[pl.BlockSpec((bm, bn), lambda i, j: (i, j))] + s_out,
        out_shape=[jax.ShapeDtypeStruct((m, n), BF16)] + s_shapes,
        compiler_params=pltpu.CompilerParams(
            dimension_semantics=("parallel", "parallel"),
            vmem_limit_bytes=_vmem_limit(nbytes)),
        name="ffn_up",
    )(hn, wg, wu, *sides)
    return [res[0], *finish(res[1:])]


def _layer(x, norm_mix_w, w_in, conv_w, a_log, dt_bias, gdn_norm_w, fox_b_f,
           fox_q_norm_w, fox_k_norm_w, w_branch_a, w_branch_b, w_out,
           norm_ffn_w, w_ffn_gate, w_ffn_up, w_ffn_down):
    b, t, d = x.shape
    m = b * t
    x2 = x.reshape(m, d)

    o_small_a = 4 * HW
    o_b = o_small_a + 2 * HEADS
    o_f = o_b + 3 * HW
    o_gate = o_f + HEADS
    w_t = w_in.T
    n_in = w_t.shape[0]
    wt_gdn, wt_ba, wt_f = _cast_rows(w_t, (0, o_small_a), [(o_small_a, o_b - o_small_a), (o_f, o_gate - o_f)], "w_in_cast")
    wt_small = jnp.pad(jnp.concatenate([wt_ba, wt_f], axis=0), ((0, LANES - 3 * HEADS), (0, 0)))

    xn = _rmsnorm(x2, norm_mix_w, "norm_mix")
    p_gdn, wg, wt_fox, wt_gate = _matmul_nt(
        xn, wt_gdn, 0, o_small_a, BF16, "in_proj_gdn",
        sides=(w_ffn_gate, (w_t, o_b, o_f - o_b), (w_t, o_gate, n_in - o_gate)))
    p_fox, = _matmul_nt(xn, wt_fox, 0, o_f - o_b, BF16, "in_proj_fox")
    p_gate, wu = _matmul_nt(xn, wt_gate, 0, n_in - o_gate, BF16, "in_proj_gate", sides=(w_ffn_up,))
    small, = _matmul_nt(xn, wt_small, 0, LANES, F32, "in_proj_small")
    p_gdn = p_gdn.reshape(b, t, 4 * HW)
    p_fox = p_fox.reshape(b, t, 3 * HW)
    small3 = small.reshape(b, t, LANES)

    o_a, wa, wb = _gdn(p_gdn, small3, conv_w, a_log, dt_bias, gdn_norm_w, sides=(w_branch_a, w_branch_b))
    qn, kn = _fox_prep(p_fox, small3, fox_b_f, fox_q_norm_w, fox_k_norm_w)
    o_b_, wo = _fox(qn, kn, p_fox, sides=(w_out,))

    merged = _merge(o_a.reshape(m, HW), o_b_.reshape(m, HW), wa, wb, p_gate)
    h = _matmul(merged, wo, F32, "out_proj", residual=x2)

    hn = _rmsnorm(h, norm_ffn_w, "norm_ffn")
    act, wd = _ffn1(hn, wg, wu, sides=(w_ffn_down,))
    out = _matmul(act, wd, F32, "ffn_down", residual=h,
                  bm_prefs=(512, 256, 128, 64, 8), bn_prefs=(512, 256, 128))
    return out.reshape(b, t, d)


def kernel(x, norm_mix_w, w_in, conv_w, a_log, dt_bias, gdn_norm_w, fox_b_f, fox_q_norm_w, fox_k_norm_w, w_branch_a, w_branch_b, w_out, norm_ffn_w, w_ffn_gate, w_ffn_up, w_ffn_down):
    h = x
    for l in range(norm_mix_w.shape[0]):
        h = _layer(h, norm_mix_w[l], w_in[l], conv_w[l], a_log[l], dt_bias[l], gdn_norm_w[l],
                   fox_b_f[l], fox_q_norm_w[l], fox_k_norm_w[l], w_branch_a[l], w_branch_b[l],
                   w_out[l], norm_ffn_w[l], w_ffn_gate[l], w_ffn_up[l], w_ffn_down[l])
    return h
```

```python
import functools

import jax
import jax.numpy as jnp
from jax import lax
from jax.experimental import pallas as pl
from jax.experimental.pallas import tpu as pltpu

F32 = jnp.float32
BF16 = jnp.bfloat16

EPS = 1e-6
HEADS = 16
DH = 128
HW = HEADS * DH
CONV_K = 4

V7X_VMEM_BYTES = 64 * 1024 * 1024
LANES = 128
SUBLANES = 8
BF16_SUBLANES = 16
LOG2E = 1.4426950408889634


def _vmem_limit(nbytes):
    return int(min(nbytes + (16 << 20), V7X_VMEM_BYTES - (4 << 20)))


def _pick(n, prefs):
    for p in prefs:
        if n % p == 0:
            return p
    return n


def _sigmoid(x):
    return 1.0 / (1.0 + jnp.exp(-x))


def _silu(x):
    return x * _sigmoid(x)


def _softplus(x):
    return jnp.maximum(x, 0.0) + jnp.log(1.0 + jnp.exp(-jnp.abs(x)))


def _dot(a, b):
    return jnp.dot(a, b, preferred_element_type=F32)


def _dot_nt(a, b):
    return lax.dot_general(a, b, (((1,), (1,)), ((), ())), preferred_element_type=F32)


def _dot_tn(a, b):
    return lax.dot_general(a, b, (((0,), (0,)), ((), ())), preferred_element_type=F32)


def _with_side_casts(body, n_in, n_out, n_side):
    def wrapped(*refs):
        ins, refs = refs[:n_in], refs[n_in:]
        side_in, refs = refs[:n_side], refs[n_side:]
        outs, refs = refs[:n_out], refs[n_out:]
        side_out, scratch = refs[:n_side], refs[n_side:]
        for si, so in zip(side_in, side_out):
            so[...] = si[...].astype(so.dtype)
        body(*ins, *outs, *scratch)
    return wrapped


def _side_window(side):
    if isinstance(side, tuple):
        return side
    return side, 0, side.shape[0]


def _side_ok(side, nsteps):
    _, _, nrows = _side_window(side)
    return nrows % nsteps == 0 and (nrows // nsteps) % BF16_SUBLANES == 0


def _split_sides(sides, nsteps):
    ok = [_side_window(s) for s in sides if _side_ok(s, nsteps)]

    def finish(casts):
        it = iter(casts)
        out = []
        for s in sides:
            w, row0, nrows = _side_window(s)
            out.append(next(it) if _side_ok(s, nsteps) else w[row0:row0 + nrows].astype(BF16))
        return out
    return [w for w, _, _ in ok], [(row0, nrows) for _, row0, nrows in ok], finish


def _side_specs(arrays, windows, nsteps, step_of):
    in_specs, out_specs, shapes, nbytes = [], [], [], 0
    for w, (row0, nrows) in zip(arrays, windows):
        r, c = nrows // nsteps, w.shape[1]
        in_specs.append(pl.BlockSpec(
            (pl.Element(r), pl.Element(c)),
            lambda *g, row0=row0, r=r: (pl.multiple_of(row0 + step_of(*g) * r, SUBLANES), 0)))
        out_specs.append(pl.BlockSpec((r, c), lambda *g: (step_of(*g), 0)))
        shapes.append(jax.ShapeDtypeStruct((nrows, c), BF16))
        nbytes += 2 * r * c * 6
    return in_specs, out_specs, shapes, nbytes


def _cast_kernel(*refs):
    n = len(refs) // 2
    for si, so in zip(refs[:n], refs[n:]):
        so[...] = si[...].astype(so.dtype)


def _cast_rows(w, main, extras, name):
    c = w.shape[1]
    row0, nrows = main
    r = _pick(nrows, (512, 256, 128, 64, 32, 16))
    in_specs = [pl.BlockSpec((pl.Element(r), pl.Element(c)),
                             lambda i: (pl.multiple_of(row0 + i * r, SUBLANES), 0))]
    out_specs = [pl.BlockSpec((r, c), lambda i: (i, 0))]
    shapes = [jax.ShapeDtypeStruct((nrows, c), BF16)]
    for e0, en in extras:
        in_specs.append(pl.BlockSpec((pl.Element(en), pl.Element(c)), lambda i, e0=e0: (e0, 0)))
        out_specs.append(pl.BlockSpec((en, c), lambda i: (0, 0)))
        shapes.append(jax.ShapeDtypeStruct((en, c), BF16))
    return pl.pallas_call(
        _cast_kernel,
        grid=(nrows // r,),
        in_specs=in_specs,
        out_specs=out_specs,
        out_shape=shapes,
        compiler_params=pltpu.CompilerParams(
            dimension_semantics=("arbitrary",),
            vmem_limit_bytes=_vmem_limit(2 * (r + sum(en for _, en in extras)) * c * 6)),
        name=name,
    )(*([w] * (1 + len(extras))))


def _rmsnorm_kernel(x_ref, w_ref, o_ref):
    x = x_ref[...]
    ms = jnp.mean(x * x, axis=-1, keepdims=True)
    o_ref[...] = (x * lax.rsqrt(ms + EPS) * w_ref[...]).astype(o_ref.dtype)


def _rmsnorm(x, w, name):
    m, d = x.shape
    bm = _pick(m, (512, 256, 128, 64, 8))
    return pl.pallas_call(
        _rmsnorm_kernel,
        grid=(m // bm,),
        in_specs=[pl.BlockSpec((bm, d), lambda i: (i, 0)),
                  pl.BlockSpec((1, d), lambda i: (0, 0))],
        out_specs=pl.BlockSpec((bm, d), lambda i: (i, 0)),
        out_shape=jax.ShapeDtypeStruct((m, d), BF16),
        compiler_params=pltpu.CompilerParams(
            dimension_semantics=("parallel",),
            vmem_limit_bytes=_vmem_limit(2 * bm * d * 6)),
        name=name,
    )(x, w.reshape(1, d).astype(F32))


def _mm_nt_kernel(x_ref, wt_ref, o_ref):
    o_ref[...] = _dot_nt(x_ref[...], wt_ref[...]).astype(o_ref.dtype)


def _matmul_nt(x, wt, row0, n, out_dtype, name, sides=()):
    m, k = x.shape
    bm = _pick(m, (1024, 512, 256, 128, 64, 8))
    bn = _pick(n, (1024, 512, 256, 128))
    nj = n // bn
    sides, wins, finish = _split_sides(sides, (m // bm) * nj)
    s_in, s_out, s_shapes, s_bytes = _side_specs(sides, wins, (m // bm) * nj, lambda i, j: i * nj + j)
    nbytes = 2 * (bm * k * 2 + k * bn * 2 + bm * bn * jnp.dtype(out_dtype).itemsize) + s_bytes
    res = pl.pallas_call(
        _with_side_casts(_mm_nt_kernel, 2, 1, len(sides)),
        grid=(m // bm, nj),
        in_specs=[pl.BlockSpec((bm, k), lambda i, j: (i, 0)),
                  pl.BlockSpec((pl.Element(bn), pl.Element(k)),
                               lambda i, j: (pl.multiple_of(row0 + j * bn, BF16_SUBLANES), 0))] + s_in,
        out_specs=[pl.BlockSpec((bm, bn), lambda i, j: (i, j))] + s_out,
        out_shape=[jax.ShapeDtypeStruct((m, n), out_dtype)] + s_shapes,
        compiler_params=pltpu.CompilerParams(
            dimension_semantics=("parallel", "parallel"),
            vmem_limit_bytes=_vmem_limit(nbytes)),
        name=name,
    )(x, wt, *sides)
    return [res[0], *finish(res[1:])]


def _mm_kernel(x_ref, w_ref, o_ref):
    o_ref[...] = _dot(x_ref[...], w_ref[...]).astype(o_ref.dtype)


def _mm_res_kernel(x_ref, w_ref, r_ref, o_ref):
    o_ref[...] = (r_ref[...] + _dot(x_ref[...], w_ref[...])).astype(o_ref.dtype)


def _matmul(x, w, out_dtype, name, residual=None, bm_prefs=(1024, 512, 256, 128, 64, 8),
            bn_prefs=(1024, 512, 256, 128)):
    m, k = x.shape
    _, n = w.shape
    bm = _pick(m, bm_prefs)
    bn = _pick(n, bn_prefs)
    osz = jnp.dtype(out_dtype).itemsize
    nbytes = 2 * (bm * k * 2 + k * bn * 2 + bm * bn * osz)
    in_specs = [pl.BlockSpec((bm, k), lambda i, j: (i, 0)),
                pl.BlockSpec((k, bn), lambda i, j: (0, j))]
    args = [x, w]
    kern = _mm_kernel
    if residual is not None:
        in_specs.append(pl.BlockSpec((bm, bn), lambda i, j: (i, j)))
        args.append(residual)
        kern = _mm_res_kernel
        nbytes += 2 * bm * bn * 4
    return pl.pallas_call(
        kern,
        grid=(m // bm, n // bn),
        in_specs=in_specs,
        out_specs=pl.BlockSpec((bm, bn), lambda i, j: (i, j)),
        out_shape=jax.ShapeDtypeStruct((m, n), out_dtype),
        compiler_params=pltpu.CompilerParams(
            dimension_semantics=("parallel", "parallel"),
            vmem_limit_bytes=_vmem_limit(nbytes)),
        name=name,
    )(*args)


GDN_CH = 128
GDN_HB = 16
GDN_HALO = 16


def _gdn_kernel(q_ref, k_ref, v_ref, z_ref, qh_ref, kh_ref, vh_ref,
                cwq_ref, cwk_ref, cwv_ref, scc_ref, scr_ref, hpr_ref, hpc_ref, gnw_ref,
                o_ref, s_ref, *, hb, rows):
    t = pl.program_id(2)

    @pl.when(t == 0)
    def _():
        s_ref[...] = jnp.zeros_like(s_ref)

    sr = lax.broadcasted_iota(jnp.int32, (3 * rows, rows + GDN_HALO), 0)
    sc = lax.broadcasted_iota(jnp.int32, (3 * rows, rows + GDN_HALO), 1)
    shift = (sc == (sr % rows) + (sr // rows) + (GDN_HALO - 3)).astype(BF16)

    def conv_silu(x_ref, h_ref, cw_ref):
        x = x_ref[...]
        halo = jnp.where(t > 0, h_ref[...], jnp.zeros_like(h_ref))
        sh = _dot(shift, jnp.concatenate([halo, x], axis=0))
        cw = cw_ref[...]
        y = x.astype(F32) * cw[CONV_K - 1:CONV_K, :]
        for i in range(CONV_K - 1):
            y = y + sh[i * rows:(i + 1) * rows] * cw[i:i + 1, :]
        return _silu(y)

    q_all = conv_silu(q_ref, qh_ref, cwq_ref)
    k_all = conv_silu(k_ref, kh_ref, cwk_ref)
    v_all = conv_silu(v_ref, vh_ref, cwv_ref)

    scc = scc_ref[...]
    scr = scr_ref[...]
    hpr = hpr_ref[...]
    hpc = hpc_ref[...]
    beta_c = _sigmoid(scc[:, 0:hb])
    g_c = -jnp.exp(hpr[0:1, :]) * _softplus(scc[:, hb:2 * hb] + hpr[1:2, :])
    g_r = -jnp.exp(hpc[:, 0:1]) * _softplus(scr[hb:2 * hb, :] + hpc[:, 1:2])

    ri = lax.broadcasted_iota(jnp.int32, (rows, rows), 0)
    ci = lax.broadcasted_iota(jnp.int32, (rows, rows), 1)
    tril_incl = ri >= ci
    tril_strict = ri > ci
    triu_incl = ri <= ci
    eye = (ri == ci).astype(F32)
    masks = []
    s = 1
    while s < rows:
        masks.append(((ri // (2 * s)) == (ci // (2 * s))) & (((ri // s) % 2) == 1) & (((ci // s) % 2) == 0))
        s *= 2
    gnw = gnw_ref[...]

    pr = []
    for hh in range(hb):
        cs = slice(hh * DH, (hh + 1) * DH)
        qc = q_all[:, cs]
        kc = k_all[:, cs]
        vc = v_all[:, cs]
        qc = qc * (lax.rsqrt(jnp.sum(qc * qc, axis=-1, keepdims=True) + EPS) * (DH ** -0.5))
        kc = kc * lax.rsqrt(jnp.sum(kc * kc, axis=-1, keepdims=True) + EPS)
        beta = beta_c[:, hh:hh + 1]
        gcol = jnp.sum(jnp.where(tril_incl, g_r[hh:hh + 1, :], 0.0), axis=1, keepdims=True)
        grow = jnp.sum(jnp.where(triu_incl, g_c[:, hh:hh + 1], 0.0), axis=0, keepdims=True)
        decay = jnp.exp(jnp.where(tril_incl, gcol - grow, -jnp.inf))
        kb = kc.astype(BF16)
        qkk = _dot_nt(jnp.concatenate([qc.astype(BF16), kb], axis=0), kb)
        qk = (qkk[0:rows] * decay).astype(BF16)
        a_strict = jnp.where(tril_strict, qkk[rows:2 * rows] * (decay * beta), 0.0)
        egc = jnp.exp(gcol)
        g_last = gcol[rows - 1:rows, :]
        pr.append(dict(
            a=a_strict.astype(BF16),
            t=eye - jnp.where(masks[0], a_strict, 0.0),
            rhs=jnp.concatenate([vc * beta, kc * (beta * egc)], axis=1).astype(BF16),
            qe=(qc * egc).astype(BF16),
            kd=(kc * jnp.exp(g_last - gcol)).astype(BF16),
            qk=qk, gl=jnp.exp(g_last)))

    for mk in masks[1:]:
        tbs = [p["t"].astype(BF16) for p in pr]
        xs = [_dot(tb, p["a"]).astype(BF16) for tb, p in zip(tbs, pr)]
        ys = [_dot(x, tb) for x, tb in zip(xs, tbs)]
        for p, y in zip(pr, ys):
            p["t"] = p["t"] - jnp.where(mk, y, 0.0)

    uws = [_dot(p["t"].astype(BF16), p["rhs"]) for p in pr]

    s_old = [s_ref[hh] for hh in range(hb)]
    wss = [_dot(jnp.concatenate([uw[:, DH:2 * DH].astype(BF16), p["qe"]], axis=0), s_h.astype(BF16))
           for uw, p, s_h in zip(uws, pr, s_old)]
    vnbs = [(uw[:, 0:DH] - ws[0:rows]).astype(BF16) for uw, ws in zip(uws, wss)]
    os_ = [ws[rows:2 * rows] + _dot(p["qk"], vnb) for ws, p, vnb in zip(wss, pr, vnbs)]
    for hh, (p, vnb, s_h) in enumerate(zip(pr, vnbs, s_old)):
        s_ref[hh] = s_h * p["gl"] + _dot_tn(p["kd"], vnb)
    for hh, o in enumerate(os_):
        cs = slice(hh * DH, (hh + 1) * DH)
        on = o * lax.rsqrt(jnp.mean(o * o, axis=-1, keepdims=True) + EPS) * gnw
        zc = z_ref[:, cs].astype(F32)
        o_ref[:, cs] = (on * _silu(zc)).astype(o_ref.dtype)


def _gdn(p3, small, conv_w, a_log, dt_bias, gdn_norm_w, sides=()):
    b, t, _ = p3.shape
    hb, rows = GDN_HB, GDN_CH
    g = HEADS // hb
    hbw = hb * DH
    nblk = HW // hbw
    bl = small[..., 0:HEADS].reshape(b, t, g, hb)
    al = small[..., HEADS:2 * HEADS].reshape(b, t, g, hb)
    scc = jnp.transpose(jnp.concatenate([bl, al], axis=-1), (0, 2, 1, 3))
    scr = jnp.transpose(scc, (0, 1, 3, 2))
    hpr = jnp.stack([a_log.reshape(g, hb), dt_bias.reshape(g, hb)], axis=1).astype(F32)
    hpc = jnp.transpose(hpr, (0, 2, 1))
    cw = conv_w.astype(F32)

    def col(off):
        return lambda bi, gi, ti: (bi, ti, off * nblk + gi)

    def halo(off):
        return lambda bi, gi, ti: (bi, jnp.maximum(ti * (rows // GDN_HALO) - 1, 0), off * nblk + gi)

    def cwmap(off):
        return lambda bi, gi, ti: (0, off * nblk + gi)

    blk = lambda off: pl.BlockSpec((None, rows, hbw), col(off))
    hblk = lambda off: pl.BlockSpec((None, GDN_HALO, hbw), halo(off))
    in_specs = [blk(0), blk(1), blk(2), blk(3), hblk(0), hblk(1), hblk(2),
                pl.BlockSpec((CONV_K, hbw), cwmap(0)),
                pl.BlockSpec((CONV_K, hbw), cwmap(1)),
                pl.BlockSpec((CONV_K, hbw), cwmap(2)),
                pl.BlockSpec((None, None, rows, 2 * hb), lambda bi, gi, ti: (bi, gi, ti, 0)),
                pl.BlockSpec((None, None, 2 * hb, rows), lambda bi, gi, ti: (bi, gi, 0, ti)),
                pl.BlockSpec((None, 2, hb), lambda bi, gi, ti: (gi, 0, 0)),
                pl.BlockSpec((None, hb, 2), lambda bi, gi, ti: (gi, 0, 0)),
                pl.BlockSpec((1, DH), lambda bi, gi, ti: (0, 0))]
    nt = t // rows
    sides, wins, finish = _split_sides(sides, b * g * nt)
    s_in, s_out, s_shapes, s_bytes = _side_specs(sides, wins, b * g * nt, lambda bi, gi, ti: (bi * g + gi) * nt + ti)
    res = pl.pallas_call(
        _with_side_casts(functools.partial(_gdn_kernel, hb=hb, rows=rows), len(in_specs), 1, len(sides)),
        grid=(b, g, nt),
        in_specs=in_specs + s_in,
        out_specs=[pl.BlockSpec((None, rows, hbw), lambda bi, gi, ti: (bi, ti, gi))] + s_out,
        out_shape=[jax.ShapeDtypeStruct((b, t, HW), BF16)] + s_shapes,
        scratch_shapes=[pltpu.VMEM((hb, DH, DH), F32)],
        compiler_params=pltpu.CompilerParams(
            dimension_semantics=("parallel", "parallel", "arbitrary"),
            vmem_limit_bytes=_vmem_limit((8 << 20) + s_bytes)),
        name="gdn",
    )(p3, p3, p3, p3, p3, p3, p3, cw, cw, cw, scc, scr, hpr, hpc,
      gdn_norm_w.reshape(1, DH).astype(F32), *sides)
    return [res[0], *finish(res[1:])]


FOX_PREP_ROWS = 512
FOX_KW = 2 * DH


def _split3(x):
    hi = x.astype(BF16)
    r1 = x - hi.astype(F32)
    mid = r1.astype(BF16)
    lo = (r1 - mid.astype(F32)).astype(BF16)
    return hi, mid, lo


def _fox_prep_kernel(q_ref, k_ref, sm_ref, bf_ref, wq_ref, wk_ref,
                     qn_ref, kn_ref, carry_ref, *, rows):
    t = pl.program_id(1)

    @pl.when(t == 0)
    def _():
        carry_ref[...] = jnp.zeros_like(carry_ref)

    x = sm_ref[:, 2 * HEADS:3 * HEADS] + bf_ref[...]
    ls = -_softplus(-x)
    ri = lax.broadcasted_iota(jnp.int32, (rows, rows), 0)
    ci = lax.broadcasted_iota(jnp.int32, (rows, rows), 1)
    tril = (ri >= ci).astype(BF16)
    hi, mid, lo = _split3(ls)
    c = _dot(tril, hi) + (_dot(tril, mid) + _dot(tril, lo)) + carry_ref[...]
    carry_ref[...] = c[rows - 1:rows, :]
    nh, nm, nl = _split3(-LOG2E * c)

    wq = wq_ref[...] * (DH ** -0.5 * LOG2E)
    wk = wk_ref[...]
    pr_i = lax.broadcasted_iota(jnp.int32, (3 * HEADS, HW), 0)
    pc_i = lax.broadcasted_iota(jnp.int32, (3 * HEADS, HW), 1)
    place = (pc_i == (pr_i % HEADS) * DH + pr_i // HEADS).astype(BF16)
    extra_all = _dot(jnp.concatenate([nh, nm, nl], axis=1), place)
    mean_mat = jnp.full((DH, DH), 1.0 / DH, BF16)
    for h in range(HEADS):
        cs = slice(h * DH, (h + 1) * DH)
        q = q_ref[:, cs].astype(F32)
        k = k_ref[:, cs].astype(F32)
        qms = _dot((q * q).astype(BF16), mean_mat)
        kms = _dot((k * k).astype(BF16), mean_mat)
        qn_ref[:, cs] = (q * lax.rsqrt(qms + EPS) * wq).astype(qn_ref.dtype)
        kn_ref[:, h * FOX_KW:h * FOX_KW + DH] = (k * lax.rsqrt(kms + EPS) * wk).astype(kn_ref.dtype)
        kn_ref[:, h * FOX_KW + DH:(h + 1) * FOX_KW] = extra_all[:, cs].astype(kn_ref.dtype)


def _fox_prep(p3, small3, fox_b_f, wq, wk):
    b, t, _ = p3.shape
    rows = _pick(t, (FOX_PREP_ROWS, 256, 128))
    return pl.pallas_call(
        functools.partial(_fox_prep_kernel, rows=rows),
        grid=(b, t // rows),
        in_specs=[pl.BlockSpec((None, rows, HW), lambda bi, ti: (bi, ti, 0)),
                  pl.BlockSpec((None, rows, HW), lambda bi, ti: (bi, ti, 1)),
                  pl.BlockSpec((None, rows, LANES), lambda bi, ti: (bi, ti, 0)),
                  pl.BlockSpec((1, HEADS), lambda bi, ti: (0, 0)),
                  pl.BlockSpec((1, DH), lambda bi, ti: (0, 0)),
                  pl.BlockSpec((1, DH), lambda bi, ti: (0, 0))],
        out_specs=[pl.BlockSpec((None, rows, HW), lambda bi, ti: (bi, ti, 0)),
                   pl.BlockSpec((None, rows, HEADS * FOX_KW), lambda bi, ti: (bi, ti, 0))],
        out_shape=[jax.ShapeDtypeStruct((b, t, HW), BF16),
                   jax.ShapeDtypeStruct((b, t, HEADS * FOX_KW), BF16)],
        scratch_shapes=[pltpu.VMEM((1, HEADS), F32)],
        compiler_params=pltpu.CompilerParams(
            dimension_semantics=("parallel", "arbitrary"),
            vmem_limit_bytes=_vmem_limit(24 << 20)),
        name="fox_prep",
    )(p3, p3, small3, fox_b_f.reshape(1, HEADS).astype(F32),
      wq.reshape(1, DH).astype(F32), wk.reshape(1, DH).astype(F32))


FOX_BQ = 2048
FOX_SUB = 256


def _fox_kernel(q_ref, k_ref, v_ref, o_ref, vp_ref, *, bq, sub):
    qi = pl.program_id(2)
    nsub = bq // sub

    @pl.when(qi == 0)
    def _():
        vp_ref[:, 0:DH] = v_ref[...]
        lane_v = lax.broadcasted_iota(jnp.int32, (vp_ref.shape[0], DH), 1)
        vp_ref[:, DH:FOX_KW] = jnp.where(lane_v == 0, 1.0, 0.0).astype(BF16)

    lane_q = lax.broadcasted_iota(jnp.int32, (sub, DH), 1)
    ones3 = jnp.where(lane_q < 3, 1.0, 0.0).astype(BF16)
    qs = [jnp.concatenate([q_ref[i * sub:(i + 1) * sub, :], ones3], axis=1) for i in range(nsub)]

    def step(carry, off, widths, masks):
        ss = [_dot_nt(qs[i], k_ref[pl.ds(off, widths[i]), :]) for i in range(nsub)]
        if masks is not None:
            ss = [jnp.where(masks[i], ss[i], -jnp.inf) for i in range(nsub)]
        ms = [jnp.maximum(carry[i][0], jnp.max(ss[i], axis=-1, keepdims=True)) for i in range(nsub)]
        ps = [jnp.exp2(ss[i] - ms[i]).astype(BF16) for i in range(nsub)]
        pvs = [_dot(ps[i], vp_ref[pl.ds(off, widths[i]), :]) for i in range(nsub)]
        return tuple((ms[i], jnp.exp2(carry[i][0] - ms[i]) * carry[i][1] + pvs[i]) for i in range(nsub))

    def body(j, carry):
        return step(carry, pl.multiple_of(j * bq, bq), [bq] * nsub, None)

    init = tuple((jnp.full((sub, 1), -jnp.inf, F32), jnp.zeros((sub, FOX_KW), F32)) for _ in range(nsub))
    carry = lax.fori_loop(0, qi, body, init)

    widths = [(i + 1) * sub for i in range(nsub)]
    masks = [(lax.broadcasted_iota(jnp.int32, (sub, w), 0) + i * sub) >= lax.broadcasted_iota(jnp.int32, (sub, w), 1)
             for i, w in enumerate(widths)]
    carry = step(carry, pl.multiple_of(qi * bq, bq), widths, masks)
    for i in range(nsub):
        acc = carry[i][1]
        o_ref[i * sub:(i + 1) * sub, :] = (acc[:, 0:DH] * (1.0 / acc[:, DH:DH + 1])).astype(o_ref.dtype)


def _fox(qn, kn, p3, sides=()):
    b, t, _ = qn.shape
    bq = _pick(t, (FOX_BQ, 256, 128))
    sub = min(FOX_SUB, bq)
    nq = t // bq
    voff = 2 * HW // DH
    sides, wins, finish = _split_sides(sides, b * HEADS * nq)
    s_in, s_out, s_shapes, s_bytes = _side_specs(sides, wins, b * HEADS * nq,
                                                 lambda bi, hi, qi: (bi * HEADS + hi) * nq + qi)
    res = pl.pallas_call(
        _with_side_casts(functools.partial(_fox_kernel, bq=bq, sub=sub), 3, 1, len(sides)),
        grid=(b, HEADS, nq),
        in_specs=[pl.BlockSpec((None, bq, DH), lambda bi, hi, qi: (bi, qi, hi)),
                  pl.BlockSpec((None, t, FOX_KW), lambda bi, hi, qi: (bi, 0, hi)),
                  pl.BlockSpec((None, t, DH), lambda bi, hi, qi: (bi, 0, voff + hi))] + s_in,
        out_specs=[pl.BlockSpec((None, bq, DH), lambda bi, hi, qi: (bi, qi, hi))] + s_out,
        out_shape=[jax.ShapeDtypeStruct((b, t, HW), BF16)] + s_shapes,
        scratch_shapes=[pltpu.VMEM((t, FOX_KW), BF16)],
        compiler_params=pltpu.CompilerParams(
            dimension_semantics=("parallel", "parallel", "arbitrary"),
            vmem_limit_bytes=_vmem_limit((40 << 20) + s_bytes)),
        name="fox",
    )(qn, kn, p3, *sides)
    return [res[0], *finish(res[1:])]


def _merge_kernel(oa_ref, ob_ref, wa_ref, wb_ref, ga_ref, gb_ref, o_ref):
    ya = _dot(oa_ref[...], wa_ref[...])
    yb = _dot(ob_ref[...], wb_ref[...])
    o_ref[...] = (_sigmoid(ga_ref[...].astype(F32)) * ya
                  + _sigmoid(gb_ref[...].astype(F32)) * yb).astype(o_ref.dtype)


def _merge(oa, ob, wa, wb, gates):
    m, k = oa.shape
    n = wa.shape[1]
    bm = _pick(m, (1024, 512, 256, 128, 64, 8))
    bn = _pick(n, (1024, 512, 256, 128))
    gb0 = n // bn
    nbytes = 2 * (2 * bm * k * 2 + 2 * k * bn * 2 + 3 * bm * bn * 2)
    return pl.pallas_call(
        _merge_kernel,
        grid=(m // bm, n // bn),
        in_specs=[pl.BlockSpec((bm, k), lambda i, j: (i, 0)),
                  pl.BlockSpec((bm, k), lambda i, j: (i, 0)),
                  pl.BlockSpec((k, bn), lambda i, j: (0, j)),
                  pl.BlockSpec((k, bn), lambda i, j: (0, j)),
                  pl.BlockSpec((bm, bn), lambda i, j: (i, j)),
                  pl.BlockSpec((bm, bn), lambda i, j: (i, gb0 + j))],
        out_specs=pl.BlockSpec((bm, bn), lambda i, j: (i, j)),
        out_shape=jax.ShapeDtypeStruct((m, n), BF16),
        compiler_params=pltpu.CompilerParams(
            dimension_semantics=("parallel", "parallel"),
            vmem_limit_bytes=_vmem_limit(nbytes)),
        name="merge",
    )(oa, ob, wa, wb, gates, gates)


def _ffn1_kernel(x_ref, wg_ref, wu_ref, o_ref):
    x = x_ref[...]
    g = _dot(x, wg_ref[...])
    u = _dot(x, wu_ref[...])
    o_ref[...] = (_silu(g) * u).astype(o_ref.dtype)


def _ffn1(hn, wg, wu, sides=()):
    m, k = hn.shape
    n = wg.shape[1]
    bm = _pick(m, (2048, 1024, 512, 256, 128, 64, 8))
    bn = _pick(n, (512, 256, 128))
    nj = n // bn
    sides, wins, finish = _split_sides(sides, (m // bm) * nj)
    s_in, s_out, s_shapes, s_bytes = _side_specs(sides, wins, (m // bm) * nj, lambda i, j: i * nj + j)
    nbytes = 2 * (bm * k * 2 + 2 * k * bn * 2 + bm * bn * 2) + 2 * bm * bn * 4 + s_bytes
    res = pl.pallas_call(
        _with_side_casts(_ffn1_kernel, 3, 1, len(sides)),
        grid=(m // bm, nj),
        in_specs=[pl.BlockSpec((bm, k), lambda i, j: (i, 0)),
                  pl.BlockSpec((k, bn), lambda i, j: (0, j)),
                  pl.BlockSpec((k, bn), lambda i, j: (0, j))] + s_in,
        out_specs=[pl.BlockSpec((bm, bn), lambda i, j: (i, j))] + s_out,
        out_shape=[jax.ShapeDtypeStruct((m, n), BF16)] + s_shapes,
        compiler_params=pltpu.CompilerParams(
            dimension_semantics=("parallel", "parallel"),
            vmem_limit_bytes=_vmem_limit(nbytes)),
        name="ffn_up",
    )(hn, wg, wu, *sides)
    return [res[0], *finish(res[1:])]


def _layer(x, norm_mix_w, w_in, conv_w, a_log, dt_bias, gdn_norm_w, fox_b_f,
           fox_q_norm_w, fox_k_norm_w, w_branch_a, w_branch_b, w_out,
           norm_ffn_w, w_ffn_gate, w_ffn_up, w_ffn_down):
    b, t, d = x.shape
    m = b * t
    x2 = x.reshape(m, d)

    o_small_a = 4 * HW
    o_b = o_small_a + 2 * HEADS
    o_f = o_b + 3 * HW
    o_gate = o_f + HEADS
    w_t = w_in.T
    n_in = w_t.shape[0]
    wt_gdn, wt_ba, wt_f = _cast_rows(w_t, (0, o_small_a), [(o_small_a, o_b - o_small_a), (o_f, o_gate - o_f)], "w_in_cast")
    wt_small = jnp.pad(jnp.concatenate([wt_ba, wt_f], axis=0), ((0, LANES - 3 * HEADS), (0, 0)))

    xn = _rmsnorm(x2, norm_mix_w, "norm_mix")
    p_gdn, wg, wt_fox, wt_gate = _matmul_nt(
        xn, wt_gdn, 0, o_small_a, BF16, "in_proj_gdn",
        sides=(w_ffn_gate, (w_t, o_b, o_f - o_b), (w_t, o_gate, n_in - o_gate)))
    p_fox, = _matmul_nt(xn, wt_fox, 0, o_f - o_b, BF16, "in_proj_fox")
    p_gate, wu = _matmul_nt(xn, wt_gate, 0, n_in - o_gate, BF16, "in_proj_gate", sides=(w_ffn_up,))
    small, = _matmul_nt(xn, wt_small, 0, LANES, F32, "in_proj_small")
    p_gdn = p_gdn.reshape(b, t, 4 * HW)
    p_fox = p_fox.reshape(b, t, 3 * HW)
    small3 = small.reshape(b, t, LANES)

    o_a, wa, wb = _gdn(p_gdn, small3, conv_w, a_log, dt_bias, gdn_norm_w, sides=(w_branch_a, w_branch_b))
    qn, kn = _fox_prep(p_fox, small3, fox_b_f, fox_q_norm_w, fox_k_norm_w)
    o_b_, wo = _fox(qn, kn, p_fox, sides=(w_out,))

    merged = _merge(o_a.reshape(m, HW), o_b_.reshape(m, HW), wa, wb, p_gate)
    h = _matmul(merged, wo, F32, "out_proj", residual=x2)

    hn = _rmsnorm(h, norm_ffn_w, "norm_ffn")
    act, wd = _ffn1(hn, wg, wu, sides=(w_ffn_down,))
    out = _matmul(act, wd, F32, "ffn_down", residual=h,
                  bm_prefs=(512, 256, 128, 64, 8), bn_prefs=(512, 256, 128))
    return out.reshape(b, t, d)


def kernel(x, norm_mix_w, w_in, conv_w, a_log, dt_bias, gdn_norm_w, fox_b_f, fox_q_norm_w, fox_k_norm_w, w_branch_a, w_branch_b, w_out, norm_ffn_w, w_ffn_gate, w_ffn_up, w_ffn_down):
    h = x
    for l in range(norm_mix_w.shape[0]):
        h = _layer(h, norm_mix_w[l], w_in[l], conv_w[l], a_log[l], dt_bias[l], gdn_norm_w[l],
                   fox_b_f[l], fox_q_norm_w[l], fox_k_norm_w[l], w_branch_a[l], w_branch_b[l],
                   w_out[l], norm_ffn_w[l], w_ffn_gate[l], w_ffn_up[l], w_ffn_down[l])
    return h
```
